```python
import math
import jax, jax.numpy as jnp
from jax import lax
import numpy as np

D_MODEL = 2048
BATCH = 8
SEQ = 2048
DEPTH = 1
DEC_BATCH = 32
DEC_SEQ = 8
PAST_LEN = 16384
PAGE_SIZE = 128

MIX_WIDTH = D_MODEL
H_SB = 8
D_SB = MIX_WIDTH // 2 // H_SB
H_DIFF = 8
D_V_DIFF = MIX_WIDTH // 2 // H_DIFF
D_QK_DIFF = D_V_DIFF // 2
W_SB = H_SB * D_SB
W_DIFF = H_DIFF * D_V_DIFF
IN_WIDTH = 3 * W_SB + 3 * W_DIFF
N_BUCKETS = 32
MAX_DISTANCE = 128
N_EXPERTS = 32
TOP_K = 4
D_FF = D_MODEL
SWIGLU_LIMIT = 7.0
SWIGLU_ALPHA = 1.702
Q_BLOCK = 128
LN_EPS = 1e-5
DEEPNORM_ALPHA = (2.0 * DEPTH) ** 0.25
DEEPNORM_BETA = (8.0 * DEPTH) ** -0.25

kernel_name = "hymba_sb_diff_moe_step"


def layer_norm(x, g=None, b=None):
    xf = x.astype(jnp.float32)
    mu = jnp.mean(xf, axis=-1, keepdims=True)
    var = jnp.mean(jnp.square(xf - mu), axis=-1, keepdims=True)
    y = (xf - mu) * lax.rsqrt(var + LN_EPS)
    if g is not None:
        y = y * g.astype(jnp.float32) + b.astype(jnp.float32)
    return y.astype(x.dtype)


def rms_norm(x, g):
    xf = x.astype(jnp.float32)
    y = xf * lax.rsqrt(jnp.mean(jnp.square(xf), axis=-1, keepdims=True) + LN_EPS)
    return (y * g.astype(jnp.float32)).astype(x.dtype)


def lambda_init(layer_idx):
    return 0.8 - 0.6 * math.exp(-0.3 * layer_idx)


def t5_bucket(rel):
    n = jnp.maximum(rel, 0)
    max_exact = N_BUCKETS // 2
    nf = jnp.maximum(n, 1).astype(jnp.float32)
    large = max_exact + (jnp.log(nf / max_exact) / math.log(MAX_DISTANCE / max_exact)
                         * (N_BUCKETS - max_exact)).astype(jnp.int32)
    large = jnp.minimum(large, N_BUCKETS - 1)
    return jnp.where(n < max_exact, n, large)


def stick_breaking_block(q, k, v, q_pos, k_pos):
    z = jnp.einsum('bqhd,bkhd->bhqk', q, k).astype(jnp.float32) * (D_SB ** -0.5)
    strict = k_pos[None, :] < q_pos[:, None]
    log_1m = jnp.where(strict, jax.nn.log_sigmoid(-z), 0.0)
    between = lax.cumsum(log_1m, axis=3, reverse=True) - log_1m
    a = jnp.where(strict, jnp.exp(jax.nn.log_sigmoid(z) + between), 0.0)
    return jnp.einsum('bhqk,bkhd->bqhd', a.astype(v.dtype), v)


def diff_block(q, k, v, q_pos, k_pos, rel_bias, lam):
    q1, q2 = q[..., :D_QK_DIFF], q[..., D_QK_DIFF:]
    k1, k2 = k[..., :D_QK_DIFF], k[..., D_QK_DIFF:]
    rel = q_pos[:, None] - k_pos[None, :]
    causal = rel >= 0
    bias = jnp.transpose(rel_bias[t5_bucket(rel)], (2, 0, 1)).astype(jnp.float32)

    def probs(qi, ki):
        s = jnp.einsum('bqhd,bkhd->bhqk', qi, ki).astype(jnp.float32) * (D_QK_DIFF ** -0.5) + bias
        return jax.nn.softmax(jnp.where(causal, s, -jnp.inf), axis=-1)

    a = probs(q1, k1) - lam * probs(q2, k2)
    return jnp.einsum('bhqk,bkhd->bqhd', a.astype(v.dtype), v)


def mixers(q_sb, k_sb, v_sb, q_d, k_d, v_d, q_pos, k_pos, rel_bias, lam):
    B, T = q_sb.shape[:2]
    qb = math.gcd(T, Q_BLOCK)
    nb = T // qb

    def to_blocks(a):
        return a.reshape(B, nb, qb, *a.shape[2:]).swapaxes(0, 1)

    def from_blocks(a):
        return a.swapaxes(0, 1).reshape(B, T, *a.shape[3:])

    def one_block(args):
        qs, qd, qp = args
        return (stick_breaking_block(qs, k_sb, v_sb, qp, k_pos),
                diff_block(qd, k_d, v_d, qp, k_pos, rel_bias, lam))

    o_sb, o_d = lax.map(one_block, (to_blocks(q_sb), to_blocks(q_d), q_pos.reshape(nb, qb)))
    return from_blocks(o_sb), from_blocks(o_d)


def moe(h, w_router, b_router, w_up, b_up, w_down, b_down):
    logits = (h @ w_router + b_router).astype(jnp.float32)
    top_v, top_i = lax.top_k(logits, TOP_K)
    gates = jax.nn.softmax(top_v, axis=-1)
    combine = jnp.sum(jax.nn.one_hot(top_i, N_EXPERTS, dtype=jnp.float32) * gates[..., None], axis=-2)
    y = jnp.zeros_like(h)
    for e in range(N_EXPERTS):
        u = h @ w_up[e] + b_up[e]
        glu = jnp.minimum(u[..., 0::2], SWIGLU_LIMIT)
        lin = jnp.clip(u[..., 1::2], -SWIGLU_LIMIT, SWIGLU_LIMIT)
        act = glu * jax.nn.sigmoid(SWIGLU_ALPHA * glu) * (lin + 1.0)
        y = y + combine[..., e:e + 1].astype(h.dtype) * (act @ w_down[e] + b_down[e])
    return y


def trunk_layer(layer_idx, x, c, past, q_pos, k_pos, p, rel_bias):
    B, T, _ = x.shape
    ada = jax.nn.silu(c) @ p['w_ada'] + p['b_ada']
    sh1, sc1, g1, sh2, sc2, g2 = jnp.split(ada[:, None, :], 6, axis=-1)

    h = layer_norm(x) * (1.0 + sc1) + sh1
    proj = h @ p['w_in']
    cuts = [W_SB, 2 * W_SB, 3 * W_SB, 3 * W_SB + W_DIFF, 3 * W_SB + 2 * W_DIFF]
    q_sb, k_sb, v_sb, q_d, k_d, v_d = jnp.split(proj, cuts, axis=-1)
    q_sb, k_sb, v_sb = (a.reshape(B, T, H_SB, D_SB) for a in (q_sb, k_sb, v_sb))
    q_d, k_d, v_d = (a.reshape(B, T, H_DIFF, D_V_DIFF) for a in (q_d, k_d, v_d))
    new_rows = (k_sb, v_sb, k_d, v_d)

    if past is None:
        ka_sb, va_sb, ka_d, va_d = k_sb, v_sb, k_d, v_d
    else:
        pk_sb, pv_sb, pk_d, pv_d = past
        ka_sb = jnp.concatenate([pk_sb, k_sb], axis=1)
        va_sb = jnp.concatenate([pv_sb, v_sb], axis=1)
        ka_d = jnp.concatenate([pk_d, k_d], axis=1)
        va_d = jnp.concatenate([pv_d, v_d], axis=1)

    lam_init = lambda_init(layer_idx)
    lam = (jnp.exp(jnp.sum(p['lambda_q1'].astype(jnp.float32) * p['lambda_k1'].astype(jnp.float32)))
           - jnp.exp(jnp.sum(p['lambda_q2'].astype(jnp.float32) * p['lambda_k2'].astype(jnp.float32)))
           + lam_init)

    o_sb, o_d = mixers(q_sb, ka_sb, va_sb, q_d, ka_d, va_d, q_pos, k_pos, rel_bias, lam)
    o_d = rms_norm(o_d, p['subln_g']) * (1.0 - lam_init)
    o = jnp.concatenate([o_sb.reshape(B, T, W_SB), o_d.reshape(B, T, W_DIFF)], axis=-1) @ p['w_out']
    x = layer_norm(DEEPNORM_ALPHA * x + g1 * o, p['ln1_g'], p['ln1_b'])

    h2 = layer_norm(x) * (1.0 + sc2) + sh2
    f = moe(h2, p['w_router'], p['b_router'], p['w_up'], p['b_up'], p['w_down'], p['b_down'])
    x = layer_norm(DEEPNORM_ALPHA * x + g2 * f, p['ln2_g'], p['ln2_b'])
    return x, new_rows


def gather_pages(cache, page_table, layer_idx):
    g = cache[page_table, :, layer_idx]
    return g.reshape(g.shape[0], g.shape[1] * g.shape[2], *g.shape[3:])


def setup_inputs(seed: int = 0) -> dict:
    key = jax.random.key(seed)
    ks = jax.random.split(key, 32)
    nrm = jax.random.normal
    n_pages = PAST_LEN // PAGE_SIZE
    n_used = DEC_BATCH * n_pages
    n_pool = n_used + max(1, n_used // 4)
    page_table = jax.random.permutation(ks[6], n_pool)[:n_used].reshape(DEC_BATCH, n_pages).astype(jnp.int32)
    return {
        "x_prompt": nrm(ks[0], (BATCH, SEQ, D_MODEL), jnp.float32),
        "x_sample": nrm(ks[1], (DEC_BATCH, DEC_SEQ, D_MODEL), jnp.float32),
        "cache_k_sb": nrm(ks[2], (n_pool, PAGE_SIZE, DEPTH, H_SB, D_SB), jnp.float32),
        "cache_v_sb": nrm(ks[3], (n_pool, PAGE_SIZE, DEPTH, H_SB, D_SB), jnp.float32),
        "cache_k_diff": nrm(ks[4], (n_pool, PAGE_SIZE, DEPTH, H_DIFF, D_V_DIFF), jnp.float32),
        "cache_v_diff": nrm(ks[5], (n_pool, PAGE_SIZE, DEPTH, H_DIFF, D_V_DIFF), jnp.float32),
        "page_table": page_table,
        "c_prompt": nrm(ks[7], (BATCH, D_MODEL), jnp.float32),
        "c_sample": nrm(ks[8], (DEC_BATCH, D_MODEL), jnp.float32),
        "w_ada": nrm(ks[9], (DEPTH, D_MODEL, 6 * D_MODEL), jnp.float32) * (0.5 * D_MODEL ** -0.5),
        "b_ada": nrm(ks[10], (DEPTH, 6 * D_MODEL), jnp.float32) * 0.02,
        "w_in": nrm(ks[11], (DEPTH, D_MODEL, IN_WIDTH), jnp.float32) * D_MODEL ** -0.5,
        "w_out": nrm(ks[12], (DEPTH, MIX_WIDTH, D_MODEL), jnp.float32) * (MIX_WIDTH ** -0.5 * DEEPNORM_BETA),
        "rel_bias": nrm(ks[13], (N_BUCKETS, H_DIFF), jnp.float32) * 0.5,
        "lambda_q1": nrm(ks[14], (DEPTH, D_QK_DIFF), jnp.float32) * 0.1,
        "lambda_k1": nrm(ks[15], (DEPTH, D_QK_DIFF), jnp.float32) * 0.1,
        "lambda_q2": nrm(ks[16], (DEPTH, D_QK_DIFF), jnp.float32) * 0.1,
        "lambda_k2": nrm(ks[17], (DEPTH, D_QK_DIFF), jnp.float32) * 0.1,
        "subln_g": 1.0 + 0.02 * nrm(ks[18], (DEPTH, D_V_DIFF), jnp.float32),
        "ln1_g": 1.0 + 0.02 * nrm(ks[19], (DEPTH, D_MODEL), jnp.float32),
        "ln1_b": 0.02 * nrm(ks[20], (DEPTH, D_MODEL), jnp.float32),
        "w_router": nrm(ks[21], (DEPTH, D_MODEL, N_EXPERTS), jnp.float32) * D_MODEL ** -0.5,
        "b_router": nrm(ks[22], (DEPTH, N_EXPERTS), jnp.float32) * 0.01,
        "w_up": nrm(ks[23], (DEPTH, N_EXPERTS, D_MODEL, 2 * D_FF), jnp.float32) * D_MODEL ** -0.5,
        "b_up": nrm(ks[24], (DEPTH, N_EXPERTS, 2 * D_FF), jnp.float32) * 0.02,
        "w_down": nrm(ks[25], (DEPTH, N_EXPERTS, D_FF, D_MODEL), jnp.float32) * (D_FF ** -0.5 * DEEPNORM_BETA),
        "b_down": nrm(ks[26], (DEPTH, N_EXPERTS, D_MODEL), jnp.float32) * 0.02,
        "ln2_g": 1.0 + 0.02 * nrm(ks[27], (DEPTH, D_MODEL), jnp.float32),
        "ln2_b": 0.02 * nrm(ks[28], (DEPTH, D_MODEL), jnp.float32),
    }


def reference(x_prompt, x_sample, cache_k_sb, cache_v_sb, cache_k_diff, cache_v_diff, page_table,
              c_prompt, c_sample, w_ada, b_ada, w_in, w_out, rel_bias, lambda_q1, lambda_k1,
              lambda_q2, lambda_k2, subln_g, ln1_g, ln1_b, w_router, b_router, w_up, b_up,
              w_down, b_down, ln2_g, ln2_b):
    t_p = x_prompt.shape[1]
    t_s = x_sample.shape[1]
    past_len = page_table.shape[1] * cache_k_sb.shape[1]
    pos_p = jnp.arange(t_p, dtype=jnp.int32)
    q_pos_s = past_len + jnp.arange(t_s, dtype=jnp.int32)
    k_pos_s = jnp.arange(past_len + t_s, dtype=jnp.int32)

    xp, xs = x_prompt, x_sample
    rows_p, rows_s = [], []
    for l in range(DEPTH):
        p = {"w_ada": w_ada[l], "b_ada": b_ada[l], "w_in": w_in[l], "w_out": w_out[l],
             "lambda_q1": lambda_q1[l], "lambda_k1": lambda_k1[l],
             "lambda_q2": lambda_q2[l], "lambda_k2": lambda_k2[l], "subln_g": subln_g[l],
             "ln1_g": ln1_g[l], "ln1_b": ln1_b[l], "w_router": w_router[l], "b_router": b_router[l],
             "w_up": w_up[l], "b_up": b_up[l], "w_down": w_down[l], "b_down": b_down[l],
             "ln2_g": ln2_g[l], "ln2_b": ln2_b[l]}
        past = (gather_pages(cache_k_sb, page_table, l), gather_pages(cache_v_sb, page_table, l),
                gather_pages(cache_k_diff, page_table, l), gather_pages(cache_v_diff, page_table, l))
        xp, rp = trunk_layer(l, xp, c_prompt, None, pos_p, pos_p, p, rel_bias)
        xs, rs = trunk_layer(l, xs, c_sample, past, q_pos_s, k_pos_s, p, rel_bias)
        rows_p.append(rp)
        rows_s.append(rs)

    def stack_rows(rows, i):
        return jnp.stack([r[i] for r in rows], axis=2)

    return (xp, xs,
            stack_rows(rows_p, 0), stack_rows(rows_p, 1), stack_rows(rows_p, 2), stack_rows(rows_p, 3),
            stack_rows(rows_s, 0), stack_rows(rows_s, 1), stack_rows(rows_s, 2), stack_rows(rows_s, 3))
```

```python
import functools
import math

import jax
import jax.numpy as jnp
from jax import lax
from jax.experimental import pallas as pl
from jax.experimental.pallas import tpu as pltpu

LN_EPS = 1e-5
TOP_K = 4
MAX_DISTANCE = 128
SWIGLU_LIMIT = 7.0
SWIGLU_ALPHA = 1.702
NEG_BIG = -1e30
LANES = 128
VMEM_LIMIT = 56 * 1024 * 1024

BF16 = jnp.bfloat16
F32 = jnp.float32


def _params(sem, vmem=VMEM_LIMIT):
    return pltpu.CompilerParams(dimension_semantics=sem, vmem_limit_bytes=vmem)


def _ln_rows(x):
    mu = jnp.mean(x, axis=-1, keepdims=True)
    xc = x - mu
    var = jnp.mean(xc * xc, axis=-1, keepdims=True)
    return xc * lax.rsqrt(var + LN_EPS)


def _dot_t(a, b):
    return lax.dot_general(a, b, (((1,), (1,)), ((), ())), preferred_element_type=F32)


def _softplus(z):
    return jnp.maximum(z, 0.0) + jnp.log(1.0 + jnp.exp(-jnp.abs(z)))


def _split_bf16(x):
    hi = x.astype(BF16)
    lo = (x - hi.astype(F32)).astype(BF16)
    return hi, lo


def _ada_kernel(c_ref, w_ref, b_ref, o_ref):
    c = c_ref[...]
    s = (c * jax.nn.sigmoid(c)).astype(BF16)
    o_ref[...] = jnp.dot(s, w_ref[...].astype(BF16), preferred_element_type=F32) + b_ref[...]


def _ada(c_all, w_ada, b_ada):
    rows, d = c_all.shape
    n = w_ada.shape[1]
    tn = min(n, 1024)
    return pl.pallas_call(
        _ada_kernel,
        grid=(n // tn,),
        in_specs=[pl.BlockSpec((rows, d), lambda j: (0, 0)),
                  pl.BlockSpec((d, tn), lambda j: (0, j)),
                  pl.BlockSpec((1, tn), lambda j: (0, j))],
        out_specs=pl.BlockSpec((rows, tn), lambda j: (0, j)),
        out_shape=jax.ShapeDtypeStruct((rows, n), F32),
        compiler_params=_params(("parallel",)),
        name="ada",
    )(c_all, w_ada, b_ada.reshape(1, n))


def _ln_proj_kernel(x_ref, sh_ref, sc_ref, w_ref, qsb_ref, ksb_ref, vsb_ref, qd_ref, kd_ref, vd_ref,
                    *, wsb, wd, sb_scale, d_scale):
    h = (_ln_rows(x_ref[...]) * (1.0 + sc_ref[...]) + sh_ref[...]).astype(BF16)

    def proj(lo, width):
        return jnp.dot(h, w_ref[:, lo:lo + width], preferred_element_type=F32)

    qsb_ref[...] = (proj(0, wsb) * sb_scale).astype(BF16)
    ksb_ref[...] = proj(wsb, wsb)
    vsb_ref[...] = proj(2 * wsb, wsb)
    qd_ref[...] = (proj(3 * wsb, wd) * d_scale).astype(BF16)
    kd_ref[...] = proj(3 * wsb + wd, wd)
    vd_ref[...] = proj(3 * wsb + 2 * wd, wd)


def _ln_proj(x, mod, mod_spec, w_in_bf, *, tm, wsb, wd, sb_scale, d_scale):
    t, d = x.shape
    kern = functools.partial(_ln_proj_kernel, wsb=wsb, wd=wd, sb_scale=sb_scale, d_scale=d_scale)
    row = lambda i: (i, 0)
    return pl.pallas_call(
        kern,
        grid=(t // tm,),
        in_specs=[pl.BlockSpec((tm, d), row), mod_spec(0), mod_spec(1),
                  pl.BlockSpec(w_in_bf.shape, lambda i: (0, 0), pipeline_mode=pl.Buffered(1))],
        out_specs=[pl.BlockSpec((tm, wsb), row)] * 3 + [pl.BlockSpec((tm, wd), row)] * 3,
        out_shape=[jax.ShapeDtypeStruct((t, wsb), BF16), jax.ShapeDtypeStruct((t, wsb), F32),
                   jax.ShapeDtypeStruct((t, wsb), F32), jax.ShapeDtypeStruct((t, wd), BF16),
                   jax.ShapeDtypeStruct((t, wd), F32), jax.ShapeDtypeStruct((t, wd), F32)],
        compiler_params=_params(("parallel",)),
        name="ln_proj",
    )(x, mod, mod, w_in_bf)


def _sb_block(q, kb, vb, u_mat, carry, strict):
    z = _dot_t(q, kb)
    sp = _softplus(z)
    log_1m = -sp
    if strict is not None:
        log_1m = jnp.where(strict, log_1m, 0.0)
    hi, lo = _split_bf16(log_1m)
    later = jnp.dot(hi, u_mat, preferred_element_type=F32) + jnp.dot(lo, u_mat, preferred_element_type=F32)
    a = jnp.exp((z - sp) + later + carry)
    if strict is not None:
        a = jnp.where(strict, a, 0.0)
    out = jnp.dot(a.astype(BF16), vb, preferred_element_type=F32)
    return out, carry + jnp.sum(log_1m, axis=1, keepdims=True)


def _softmax_step(s_list, v_list, m_old, l_old, acc_old):
    mx = functools.reduce(jnp.maximum, [jnp.max(s, axis=1, keepdims=True) for s in s_list])
    m_new = jnp.maximum(m_old, mx)
    alpha = jnp.exp(m_old - m_new)
    l_new = alpha * l_old
    acc = alpha * acc_old
    for s, v in zip(s_list, v_list):
        p = jnp.exp(s - m_new)
        l_new = l_new + jnp.sum(p, axis=1, keepdims=True)
        acc = acc + jnp.dot(p.astype(BF16), v, preferred_element_type=F32)
    return m_new, l_new, acc


def _sub_rms(o, g, post_scale):
    ms = jnp.mean(o * o, axis=-1, keepdims=True)
    return o * lax.rsqrt(ms + LN_EPS) * g * post_scale


def _attn_prompt_kernel(far_ref, lam_ref,
                        qsb_ref, ksb_ref, vsb_ref, qd_ref, kd_ref, vd_ref, bias_ref, u_ref, g_ref,
                        osb_ref, od_ref,
                        kbs_ref, vbs_ref, kbd_ref, vbd_ref,
                        c_ref, accs_ref, m1_ref, l1_ref, a1_ref, m2_ref, l2_ref, a2_ref,
                        *, bq, dqk, post_scale):
    h = pl.program_id(1)
    qi = pl.program_id(2)

    @pl.when(qi == 0)
    def _():
        kbs_ref[...] = ksb_ref[...].astype(BF16)
        vbs_ref[...] = vsb_ref[...].astype(BF16)
        kbd_ref[...] = kd_ref[...].astype(BF16)
        vbd_ref[...] = vd_ref[...].astype(BF16)

    q_sb = qsb_ref[...]
    q_d = qd_ref[...]
    lane = lax.broadcasted_iota(jnp.int32, q_d.shape, 1)
    q1 = jnp.where(lane < dqk, q_d, jnp.zeros_like(q_d))
    q2 = jnp.where(lane >= dqk, q_d, jnp.zeros_like(q_d))
    u_mat = u_ref[...]

    c_ref[...] = jnp.zeros_like(c_ref)
    accs_ref[...] = jnp.zeros_like(accs_ref)
    for m_ref, l_ref, a_ref in ((m1_ref, l1_ref, a1_ref), (m2_ref, l2_ref, a2_ref)):
        m_ref[...] = jnp.full_like(m_ref, NEG_BIG)
        l_ref[...] = jnp.zeros_like(l_ref)
        a_ref[...] = jnp.zeros_like(a_ref)

    def block(kstart, bias, diag):
        ks = pl.ds(kstart, bq)
        if diag:
            r = lax.broadcasted_iota(jnp.int32, (bq, bq), 0)
            s = lax.broadcasted_iota(jnp.int32, (bq, bq), 1)
            strict, causal = s < r, s <= r
        else:
            strict = causal = None
        out, carry = _sb_block(q_sb, kbs_ref[ks, :], vbs_ref[ks, :], u_mat, c_ref[...], strict)
        accs_ref[...] += out
        c_ref[...] = carry
        kd = kbd_ref[ks, :]
        vd = vbd_ref[ks, :]
        for qm, m_ref, l_ref, a_ref in ((q1, m1_ref, l1_ref, a1_ref), (q2, m2_ref, l2_ref, a2_ref)):
            sc = _dot_t(qm, kd) + bias
            if causal is not None:
                sc = jnp.where(causal, sc, NEG_BIG)
            m_new, l_new, acc = _softmax_step([sc], [vd], m_ref[...], l_ref[...], a_ref[...])
            m_ref[...] = m_new
            l_ref[...] = l_new
            a_ref[...] = acc

    block(pl.multiple_of(qi * bq, bq), bias_ref[0], True)

    @pl.when(qi >= 1)
    def _():
        block(pl.multiple_of((qi - 1) * bq, bq), bias_ref[1], False)

    far = far_ref[h]

    def far_body(t, carry):
        block(pl.multiple_of((qi - 2 - t) * bq, bq), far, False)
        return carry

    lax.fori_loop(0, jnp.maximum(qi - 1, 0), far_body, 0)

    osb_ref[...] = accs_ref[...].astype(osb_ref.dtype)
    o_d = a1_ref[...] / l1_ref[...] - lam_ref[0] * (a2_ref[...] / l2_ref[...])
    od_ref[...] = _sub_rms(o_d, g_ref[...], post_scale).astype(od_ref.dtype)


def _attn_prompt(q_sb, k_sb, v_sb, q_d, k_d, v_d, bias_tiles, far_bias, lam, u_mat, subln_g,
                 *, batch, seq, heads, bq, post_scale):
    t, w = q_sb.shape
    dh = w // heads
    nq = seq // bq
    kern = functools.partial(_attn_prompt_kernel, bq=bq, dqk=dh // 2, post_scale=post_scale)
    qmap = lambda b, h, i, *_: (b * nq + i, h)
    kvmap = lambda b, h, i, *_: (b, h)
    grid_spec = pltpu.PrefetchScalarGridSpec(
        num_scalar_prefetch=2,
        grid=(batch, heads, nq),
        in_specs=[pl.BlockSpec((bq, dh), qmap), pl.BlockSpec((seq, dh), kvmap), pl.BlockSpec((seq, dh), kvmap),
                  pl.BlockSpec((bq, dh), qmap), pl.BlockSpec((seq, dh), kvmap), pl.BlockSpec((seq, dh), kvmap),
                  pl.BlockSpec((None, 2, bq, bq), lambda b, h, i, *_: (h, 0, 0, 0)),
                  pl.BlockSpec((bq, bq), lambda b, h, i, *_: (0, 0)),
                  pl.BlockSpec((1, dh), lambda b, h, i, *_: (0, 0))],
        out_specs=[pl.BlockSpec((bq, dh), qmap), pl.BlockSpec((bq, dh), qmap)],
        scratch_shapes=[pltpu.VMEM((seq, dh), BF16)] * 4
        + [pltpu.VMEM((bq, 1), F32), pltpu.VMEM((bq, dh), F32)]
        + [pltpu.VMEM((bq, 1), F32), pltpu.VMEM((bq, 1), F32), pltpu.VMEM((bq, dh), F32)] * 2,
    )
    return pl.pallas_call(
        kern,
        grid_spec=grid_spec,
        out_shape=[jax.ShapeDtypeStruct((t, w), BF16), jax.ShapeDtypeStruct((t, w), BF16)],
        compiler_params=_params(("parallel", "parallel", "arbitrary")),
        name="attn_prompt",
    )(far_bias, lam, q_sb, k_sb, v_sb, q_d, k_d, v_d, bias_tiles, u_mat, subln_g)


def _attn_sample_kernel(pt_ref, lam_ref,
                        qs_ref, qd_ref, kns_ref, vns_ref, knd_ref, vnd_ref,
                        bnew_ref, blast_ref, bfar_ref, u_ref, g_ref, *rest,
                        n_pg, heads, dec_seq, post_scale):
    pages = rest[:4 * n_pg]
    osb_ref, od_ref = rest[4 * n_pg:4 * n_pg + 2]
    c_ref, accs_ref, m_ref, l_ref, accd_ref = rest[4 * n_pg + 2:]
    ks_pages, vs_pages = pages[0:n_pg], pages[n_pg:2 * n_pg]
    kd_pages, vd_pages = pages[2 * n_pg:3 * n_pg], pages[3 * n_pg:4 * n_pg]
    j = pl.program_id(1)
    nj = pl.num_programs(1)
    rows = heads * dec_seq
    dh = g_ref.shape[1]

    qs = qs_ref[...]
    qd = qd_ref[...]
    u_mat = u_ref[...]

    @pl.when(j == 0)
    def _():
        n_new = kns_ref.shape[0]
        t_s = lax.broadcasted_iota(jnp.int32, (rows, n_new), 0) % dec_seq
        s_s = lax.broadcasted_iota(jnp.int32, (rows, n_new), 1)
        out, carry = _sb_block(qs, kns_ref[...], vns_ref[...], u_mat,
                               jnp.zeros((rows, 1), F32), s_s < t_s)
        accs_ref[...] = out
        c_ref[...] = carry
        t_d = lax.broadcasted_iota(jnp.int32, (2 * rows, n_new), 0) % dec_seq
        s_d = lax.broadcasted_iota(jnp.int32, (2 * rows, n_new), 1)
        sc = jnp.where(s_d <= t_d, _dot_t(qd, knd_ref[...]) + bnew_ref[...], NEG_BIG)
        m_new, l_new, acc = _softmax_step(
            [sc], [vnd_ref[...]], jnp.full((2 * rows, 1), NEG_BIG, F32),
            jnp.zeros((2 * rows, 1), F32), jnp.zeros(accd_ref.shape, F32))
        m_ref[...] = m_new
        l_ref[...] = l_new
        accd_ref[...] = acc

    carry = c_ref[...]
    acc_s = accs_ref[...]
    for p in range(n_pg):
        out, carry = _sb_block(qs, ks_pages[p][...].astype(BF16), vs_pages[p][...].astype(BF16),
                               u_mat, carry, None)
        acc_s = acc_s + out
    accs_ref[...] = acc_s
    c_ref[...] = carry

    s_list, v_list = [], []
    for p in range(n_pg):
        bias = bfar_ref[...]
        if p == 0:
            bias = jnp.where(j == 0, blast_ref[...], bias)
        s_list.append(_dot_t(qd, kd_pages[p][...].astype(BF16)) + bias)
        v_list.append(vd_pages[p][...].astype(BF16))
    m_new, l_new, acc = _softmax_step(s_list, v_list, m_ref[...], l_ref[...], accd_ref[...])
    m_ref[...] = m_new
    l_ref[...] = l_new
    accd_ref[...] = acc

    @pl.when(j == nj - 1)
    def _():
        acc_s = accs_ref[...]
        dn = accd_ref[...] / l_ref[...]
        o_d = dn[:rows] - lam_ref[0] * dn[rows:]
        g = g_ref[...]
        for hh in range(heads):
            rs = slice(hh * dec_seq, (hh + 1) * dec_seq)
            cs = slice(hh * dh, (hh + 1) * dh)
            osb_ref[:, cs] = acc_s[rs, cs].astype(osb_ref.dtype)
            od_ref[:, cs] = _sub_rms(o_d[rs, cs], g, post_scale).astype(od_ref.dtype)


def _attn_sample(page_table, lam, q_sb_bd, q_d_bd, new_kv, caches, bias_new, bias_last, bias_far, u_mat, subln_g,
                 *, layer, heads, dec_seq, n_pg, post_scale):
    nb, n_pages = page_table.shape
    rows, w = q_sb_bd.shape[1:]
    n_new = new_kv[0].shape[1]
    page = caches[0].shape[1]
    dh = w // heads
    nj = n_pages // n_pg
    kern = functools.partial(_attn_sample_kernel, n_pg=n_pg, heads=heads, dec_seq=dec_seq, post_scale=post_scale)
    bmap = lambda b, j, *_: (b, 0, 0)
    const = lambda b, j, *_: (0, 0)

    def page_spec(p):
        def imap(b, j, pt, lam_):
            return (pt[b * n_pages + (n_pages - 1 - (j * n_pg + p))], 0, layer)
        return pl.BlockSpec((None, page, w), imap)

    in_specs = [pl.BlockSpec((None, rows, w), bmap), pl.BlockSpec((None, 2 * rows, w), bmap)]
    in_specs += [pl.BlockSpec((None, n_new, w), bmap)] * 4
    in_specs += [pl.BlockSpec((2 * rows, n_new), const), pl.BlockSpec((2 * rows, page), const),
                 pl.BlockSpec((2 * rows, page), const), pl.BlockSpec((page, page), const),
                 pl.BlockSpec((1, dh), const)]
    for _ in range(4):
        in_specs += [page_spec(p) for p in range(n_pg)]
    omap = lambda b, j, *_: (b, 0)
    grid_spec = pltpu.PrefetchScalarGridSpec(
        num_scalar_prefetch=2,
        grid=(nb, nj),
        in_specs=in_specs,
        out_specs=[pl.BlockSpec((dec_seq, w), omap), pl.BlockSpec((dec_seq, w), omap)],
        scratch_shapes=[pltpu.VMEM((rows, 1), F32), pltpu.VMEM((rows, w), F32),
                        pltpu.VMEM((2 * rows, 1), F32), pltpu.VMEM((2 * rows, 1), F32),
                        pltpu.VMEM((2 * rows, w), F32)],
    )
    page_args = []
    for c in caches:
        page_args += [c] * n_pg
    return pl.pallas_call(
        kern,
        grid_spec=grid_spec,
        out_shape=[jax.ShapeDtypeStruct((nb * dec_seq, w), F32)] * 2,
        compiler_params=_params(("parallel", "arbitrary")),
        name="attn_sample",
    )(page_table.reshape(-1), lam, q_sb_bd, q_d_bd, *new_kv, bias_new, bias_last, bias_far, u_mat, subln_g,
      *page_args)


def _out_router_kernel(osb_ref, od_ref, x_ref, g1_ref, sh2_ref, sc2_ref, wo_ref, lng_ref, lnb_ref,
                       wrh_ref, wrl_ref, br_ref, x1_ref, h2_ref, idx_ref, gate_ref, *, wsb, alpha):
    o = (jnp.dot(osb_ref[...].astype(BF16), wo_ref[:wsb, :], preferred_element_type=F32)
         + jnp.dot(od_ref[...].astype(BF16), wo_ref[wsb:, :], preferred_element_type=F32))
    x1 = _ln_rows(alpha * x_ref[...] + g1_ref[...] * o) * lng_ref[...] + lnb_ref[...]
    x1_ref[...] = x1
    h2 = _ln_rows(x1) * (1.0 + sc2_ref[...]) + sh2_ref[...]
    h2_ref[...] = h2.astype(h2_ref.dtype)
    hi, lo = _split_bf16(h2)
    logits = (jnp.dot(hi, wrh_ref[...], preferred_element_type=F32)
              + jnp.dot(lo, wrh_ref[...], preferred_element_type=F32)
              + jnp.dot(hi, wrl_ref[...], preferred_element_type=F32)) + br_ref[...]
    lane = lax.broadcasted_iota(jnp.int32, logits.shape, 1)
    lane_f = lane.astype(F32)
    vals, idxs = [], []
    for _ in range(TOP_K):
        mx = jnp.max(logits, axis=1, keepdims=True)
        ix = jnp.min(jnp.where(logits == mx, lane_f, float(logits.shape[1])), axis=1, keepdims=True)
        vals.append(mx)
        idxs.append(ix.astype(jnp.int32))
        logits = jnp.where(lane_f == ix, -jnp.inf, logits)
    exps = [jnp.exp(v - vals[0]) for v in vals]
    denom = functools.reduce(lambda a, b: a + b, exps)
    idx_out = jnp.zeros(lane.shape, jnp.int32)
    gate_out = jnp.zeros(lane.shape, F32)
    for k in range(TOP_K):
        idx_out = jnp.where(lane == k, idxs[k], idx_out)
        gate_out = jnp.where(lane == k, exps[k] / denom, gate_out)
    idx_ref[...] = idx_out
    gate_ref[...] = gate_out


def _out_router(o_sb, o_d, x, mod, mod_spec, w_out_bf, ln_g, ln_b, wr_hi, wr_lo, br_pad, *, tm, alpha):
    t, d = x.shape
    wsb = o_sb.shape[1]
    ne = wr_hi.shape[1]
    row = lambda i: (i, 0)
    const = lambda i: (0, 0)
    kern = functools.partial(_out_router_kernel, wsb=wsb, alpha=alpha)
    return pl.pallas_call(
        kern,
        grid=(t // tm,),
        in_specs=[pl.BlockSpec((tm, wsb), row), pl.BlockSpec((tm, o_d.shape[1]), row), pl.BlockSpec((tm, d), row),
                  mod_spec(2), mod_spec(3), mod_spec(4),
                  pl.BlockSpec(w_out_bf.shape, const, pipeline_mode=pl.Buffered(1)),
                  pl.BlockSpec((1, d), const), pl.BlockSpec((1, d), const),
                  pl.BlockSpec((d, ne), const), pl.BlockSpec((d, ne), const), pl.BlockSpec((1, ne), const)],
        out_specs=[pl.BlockSpec((tm, d), row), pl.BlockSpec((tm, d), row),
                   pl.BlockSpec((tm, ne), row), pl.BlockSpec((tm, ne), row)],
        out_shape=[jax.ShapeDtypeStruct((t, d), F32), jax.ShapeDtypeStruct((t, d), BF16),
                   jax.ShapeDtypeStruct((t, ne), jnp.int32), jax.ShapeDtypeStruct((t, ne), F32)],
        compiler_params=_params(("parallel",)),
        name="out_router",
    )(o_sb, o_d, x, mod, mod, mod, w_out_bf, ln_g, ln_b, wr_hi, wr_lo, br_pad)


def _moe_up_kernel(te_ref, tf_ref, rb_ref, nu_ref, x_ref, w_ref, b_ref, o_ref, wb_ref):
    i = pl.program_id(1)

    @pl.when(tf_ref[i] == 1)
    def _():
        wb_ref[...] = w_ref[...].astype(BF16)

    @pl.when(i < nu_ref[0])
    def _():
        u = jnp.dot(x_ref[...], wb_ref[...], preferred_element_type=F32) + b_ref[...]
        tm = u.shape[0]
        even = (lax.broadcasted_iota(jnp.int32, (tm, LANES), 1) & 1) == 0
        for c in range(u.shape[1] // (2 * LANES)):
            a = u[:, 2 * c * LANES:(2 * c + 1) * LANES]
            b = u[:, (2 * c + 1) * LANES:(2 * c + 2) * LANES]
            glu = jnp.where(even, a, pltpu.roll(b, 1, 1))
            lin = jnp.where(even, pltpu.roll(a, LANES - 1, 1), b)
            glu = jnp.minimum(glu, SWIGLU_LIMIT)
            lin = jnp.clip(lin, -SWIGLU_LIMIT, SWIGLU_LIMIT)
            act = glu * jax.nn.sigmoid(SWIGLU_ALPHA * glu) * (lin + 1.0)
            o_ref[:, c * LANES:(c + 1) * LANES] = act.astype(o_ref.dtype)


def _moe_up(tile_expert, tile_first, row_block, n_used, xs, w_up, b_up, *, tm, tf):
    r_pad, d = xs.shape
    ne, _, f2 = w_up.shape
    f = f2 // 2
    n_tiles = r_pad // tm
    grid_spec = pltpu.PrefetchScalarGridSpec(
        num_scalar_prefetch=4,
        grid=(f // tf, n_tiles),
        in_specs=[pl.BlockSpec((tm, d), lambda j, i, te, tfr, rb, nu: (rb[i], 0)),
                  pl.BlockSpec((None, d, 2 * tf), lambda j, i, te, tfr, rb, nu: (te[i], 0, j)),
                  pl.BlockSpec((None, 1, 2 * tf), lambda j, i, te, tfr, rb, nu: (te[i], 0, j))],
        out_specs=pl.BlockSpec((tm, tf), lambda j, i, te, tfr, rb, nu: (rb[i], j)),
        scratch_shapes=[pltpu.VMEM((d, 2 * tf), BF16)],
    )
    return pl.pallas_call(
        _moe_up_kernel,
        grid_spec=grid_spec,
        out_shape=jax.ShapeDtypeStruct((r_pad, f), BF16),
        compiler_params=_params(("arbitrary", "arbitrary")),
        name="moe_up",
    )(tile_expert, tile_first, row_block, n_used, xs, w_up, b_up.reshape(ne, 1, f2))


def _moe_down_kernel(te_ref, rb_ref, nu_ref, a_ref, w_ref, b_ref, g_ref, o_ref):
    i = pl.program_id(1)

    @pl.when(i < nu_ref[0])
    def _():
        y = jnp.dot(a_ref[...], w_ref[...], preferred_element_type=F32) + b_ref[...]
        o_ref[...] = (y * g_ref[...]).astype(o_ref.dtype)


def _moe_down(tile_expert, row_block, n_used, act, w_down_p, b_down, row_gate, *, tm, tn):
    r_pad, f = act.shape
    ne, _, d = w_down_p.shape
    n_tiles = r_pad // tm
    grid_spec = pltpu.PrefetchScalarGridSpec(
        num_scalar_prefetch=3,
        grid=(d // tn, n_tiles),
        in_specs=[pl.BlockSpec((tm, f), lambda j, i, te, rb, nu: (rb[i], 0)),
                  pl.BlockSpec((None, f, tn), lambda j, i, te, rb, nu: (te[i], 0, j)),
                  pl.BlockSpec((None, 1, tn), lambda j, i, te, rb, nu: (te[i], 0, j)),
                  pl.BlockSpec((tm, 1), lambda j, i, te, rb, nu: (rb[i], 0))],
        out_specs=pl.BlockSpec((tm, tn), lambda j, i, te, rb, nu: (rb[i], j)),
    )
    return pl.pallas_call(
        _moe_down_kernel,
        grid_spec=grid_spec,
        out_shape=jax.ShapeDtypeStruct((r_pad, d), BF16),
        compiler_params=_params(("arbitrary", "arbitrary")),
        name="moe_down",
    )(tile_expert, row_block, n_used, act, w_down_p, b_down.reshape(ne, 1, d), row_gate)


def _route(top_i, gates, n_exp, tm, n_tiles):
    t, k = top_i.shape
    flat_e = top_i.reshape(-1)
    onehot = (flat_e[:, None] == jnp.arange(n_exp, dtype=jnp.int32)[None, :]).astype(jnp.int32)
    csum = jnp.cumsum(onehot, axis=0)
    rank = jnp.sum((csum - onehot) * onehot, axis=1)
    counts = csum[-1]
    tiles_e = (counts + tm - 1) // tm
    tile_end = jnp.cumsum(tiles_e)
    tile_start = tile_end - tiles_e
    n_used = tile_end[-1]
    row_p = tile_start[flat_e] * tm + rank
    tile_ids = jnp.arange(n_tiles, dtype=jnp.int32)
    clamped = jnp.minimum(tile_ids, n_used - 1)
    tile_expert = jnp.minimum(jnp.searchsorted(tile_end, clamped, side="right"), n_exp - 1).astype(jnp.int32)
    tile_first = ((tile_ids == tile_start[tile_expert]) & (tile_ids < n_used)).astype(jnp.int32)
    pair_token = jnp.arange(t * k, dtype=jnp.int32) // k
    row_token = jnp.zeros((n_tiles * tm,), jnp.int32).at[row_p].set(pair_token)
    row_gate = jnp.zeros((n_tiles * tm,), F32).at[row_p].set(gates.reshape(-1))
    return (tile_expert, tile_first, clamped.astype(jnp.int32), n_used.reshape(1).astype(jnp.int32),
            row_token, row_gate.reshape(-1, 1), row_p.reshape(t, k))


def _final_kernel(x_ref, f_ref, g2_ref, lng_ref, lnb_ref, o_ref, *, alpha):
    y = alpha * x_ref[...] + g2_ref[...] * f_ref[...].astype(F32)
    o_ref[...] = _ln_rows(y) * lng_ref[...] + lnb_ref[...]


def _final_ln(x1, f, mod, mod_spec, ln_g, ln_b, *, tm, alpha):
    t, d = x1.shape
    row = lambda i: (i, 0)
    const = lambda i: (0, 0)
    return pl.pallas_call(
        functools.partial(_final_kernel, alpha=alpha),
        grid=(t // tm,),
        in_specs=[pl.BlockSpec((tm, d), row), pl.BlockSpec((tm, d), row), mod_spec(5),
                  pl.BlockSpec((1, d), const), pl.BlockSpec((1, d), const)],
        out_specs=pl.BlockSpec((tm, d), row),
        out_shape=jax.ShapeDtypeStruct((t, d), F32),
        compiler_params=_params(("parallel",)),
        name="final_ln",
    )(x1, f, mod, ln_g, ln_b)


def _t5_bucket(rel, n_buckets):
    n = jnp.maximum(rel, 0)
    max_exact = n_buckets // 2
    nf = jnp.maximum(n, 1).astype(F32)
    large = max_exact + (jnp.log(nf / max_exact) / math.log(MAX_DISTANCE / max_exact)
                         * (n_buckets - max_exact)).astype(jnp.int32)
    large = jnp.minimum(large, n_buckets - 1)
    return jnp.where(n < max_exact, n, large)


def _bias_tile(rel, rel_bias):
    return jnp.transpose(rel_bias[_t5_bucket(rel, rel_bias.shape[0])], (2, 0, 1)).astype(F32)


def _later_matrix(n):
    i = jnp.arange(n)
    return (i[:, None] > i[None, :]).astype(BF16)


def _block_diag_q(q, nb, dec_seq, heads, lane_lo, lane_hi):
    w = q.shape[1]
    dh = w // heads
    q4 = q.reshape(nb, dec_seq, heads, dh).transpose(0, 2, 1, 3)
    lane = jnp.arange(dh)
    q4 = jnp.where((lane >= lane_lo) & (lane < lane_hi), q4, jnp.zeros_like(q4))
    eye = jnp.eye(heads, dtype=q.dtype)
    out = q4[:, :, :, None, :] * eye[None, :, None, :, None]
    return out.reshape(nb, heads * dec_seq, w)


def kernel(x_prompt, x_sample, cache_k_sb, cache_v_sb, cache_k_diff, cache_v_diff, page_table, c_prompt, c_sample,
           w_ada, b_ada, w_in, w_out, rel_bias, lambda_q1, lambda_k1, lambda_q2, lambda_k2, subln_g, ln1_g, ln1_b,
           w_router, b_router, w_up, b_up, w_down, b_down, ln2_g, ln2_b):
    bp, seq, d = x_prompt.shape
    bs, dec_seq, _ = x_sample.shape
    n_pool, page, depth, h_sb, d_sb = cache_k_sb.shape
    _, _, _, h_d, d_vd = cache_k_diff.shape
    n_pages = page_table.shape[1]
    past_len = n_pages * page
    wsb, wd = h_sb * d_sb, h_d * d_vd
    n_exp = w_router.shape[-1]
    n_buckets = rel_bias.shape[0]
    d_ff = w_down.shape[2]
    alpha = (2.0 * depth) ** 0.25
    tp, ts = bp * seq, bs * dec_seq
    assert h_sb == h_d and d_sb == d_vd == LANES and page >= MAX_DISTANCE
    heads = h_sb

    tm_p = min(256, seq)
    tm_s = ts
    bq = min(256, seq)
    n_pg = min(4, n_pages)
    tm_e = min(256, ts)
    tf_up = min(512, d_ff)
    tn_down = min(1024, d)
    assert seq % bq == 0 and bq >= MAX_DISTANCE and n_pages % n_pg == 0 and (tp + ts) % tm_e == 0

    xp = x_prompt.reshape(tp, d)
    xs = x_sample.reshape(ts, d)
    c_all = jnp.concatenate([c_prompt, c_sample, jnp.zeros(((-(bp + bs)) % 16, d), F32)], axis=0)

    caches = [c.reshape(n_pool, page, depth * wsb) for c in (cache_k_sb, cache_v_sb)]
    caches += [c.reshape(n_pool, page, depth * wd) for c in (cache_k_diff, cache_v_diff)]

    ar = jnp.arange
    rel_blk = ar(bq)[:, None] - ar(bq)[None, :]
    bias_prompt = jnp.stack([_bias_tile(rel_blk, rel_bias), _bias_tile(rel_blk + bq, rel_bias)], axis=1)
    far_bias = rel_bias[n_buckets - 1].astype(F32)
    n_new = page
    t_rows = jnp.tile(ar(dec_seq), heads)
    h_rows = jnp.repeat(ar(heads), dec_seq)
    pick = lambda tile: jnp.tile(tile[h_rows, ar(heads * dec_seq)], (2, 1))
    bias_new = pick(_bias_tile(t_rows[:, None] - ar(n_new)[None, :], rel_bias))
    bias_last = pick(_bias_tile(page + t_rows[:, None] - ar(page)[None, :], rel_bias))
    bias_far = jnp.tile(jnp.broadcast_to(far_bias[h_rows][:, None], (heads * dec_seq, page)), (2, 1))
    u_blk = _later_matrix(bq)
    u_page = _later_matrix(page)

    w_in_bf = w_in.astype(BF16)
    w_out_bf = w_out.astype(BF16)

    prompt_rows, sample_rows = [], []
    for l in range(depth):
        ada = _ada(c_all, w_ada[l], b_ada[l])
        mod_p = ada[:bp].reshape(bp, 1, 6 * d)
        mod_s = jnp.repeat(ada[bp:bp + bs], dec_seq, axis=0).reshape(1, ts, 6 * d)
        tiles_per_b = seq // tm_p
        spec_p = lambda k: pl.BlockSpec((None, 1, d), lambda i, k=k: (i // tiles_per_b, 0, k))
        spec_s = lambda k: pl.BlockSpec((None, tm_s, d), lambda i, k=k: (0, i, k))

        lam_init = 0.8 - 0.6 * math.exp(-0.3 * l)
        lam = (jnp.exp(jnp.sum(lambda_q1[l].astype(F32) * lambda_k1[l].astype(F32)))
               - jnp.exp(jnp.sum(lambda_q2[l].astype(F32) * lambda_k2[l].astype(F32))) + lam_init).reshape(1)
        g_sub = subln_g[l].reshape(1, d_vd).astype(F32)
        post = 1.0 - lam_init

        proj_kw = dict(wsb=wsb, wd=wd, sb_scale=d_sb ** -0.5, d_scale=(d_vd // 2) ** -0.5)
        qsb_p, ksb_p, vsb_p, qd_p, kd_p, vd_p = _ln_proj(xp, mod_p, spec_p, w_in_bf[l], tm=tm_p, **proj_kw)
        qsb_s, ksb_s, vsb_s, qd_s, kd_s, vd_s = _ln_proj(xs, mod_s, spec_s, w_in_bf[l], tm=tm_s, **proj_kw)
        prompt_rows.append((ksb_p, vsb_p, kd_p, vd_p))
        sample_rows.append((ksb_s, vsb_s, kd_s, vd_s))

        osb_p, od_p = _attn_prompt(qsb_p, ksb_p, vsb_p, qd_p, kd_p, vd_p, bias_prompt, far_bias, lam, u_blk, g_sub,
                                   batch=bp, seq=seq, heads=heads, bq=bq, post_scale=post)

        pad_new = lambda a: jnp.pad(a.reshape(bs, dec_seq, -1).astype(BF16), ((0, 0), (0, n_new - dec_seq), (0, 0)))
        q_sb_bd = _block_diag_q(qsb_s, bs, dec_seq, heads, 0, d_sb)
        q_d_bd = jnp.concatenate([_block_diag_q(qd_s, bs, dec_seq, heads, 0, d_vd // 2),
                                  _block_diag_q(qd_s, bs, dec_seq, heads, d_vd // 2, d_vd)], axis=1)
        osb_s, od_s = _attn_sample(page_table, lam, q_sb_bd, q_d_bd,
                                   [pad_new(a) for a in (ksb_s, vsb_s, kd_s, vd_s)], caches,
                                   bias_new, bias_last, bias_far, u_page, g_sub,
                                   layer=l, heads=heads, dec_seq=dec_seq, n_pg=n_pg, post_scale=post)

        ne_pad = LANES
        wr = jnp.pad(w_router[l].astype(F32), ((0, 0), (0, ne_pad - n_exp)))
        wr_hi = wr.astype(BF16)
        wr_lo = (wr - wr_hi.astype(F32)).astype(BF16)
        br_pad = jnp.pad(b_router[l].astype(F32), (0, ne_pad - n_exp), constant_values=NEG_BIG).reshape(1, ne_pad)
        ln1 = (ln1_g[l].reshape(1, d), ln1_b[l].reshape(1, d))
        x1_p, h2_p, idx_p, gate_p = _out_router(osb_p, od_p, xp, mod_p, spec_p, w_out_bf[l], *ln1,
                                                wr_hi, wr_lo, br_pad, tm=tm_p, alpha=alpha)
        x1_s, h2_s, idx_s, gate_s = _out_router(osb_s, od_s, xs, mod_s, spec_s, w_out_bf[l], *ln1,
                                                wr_hi, wr_lo, br_pad, tm=tm_s, alpha=alpha)

        h2 = jnp.concatenate([h2_p, h2_s], axis=0)
        top_i = jnp.concatenate([idx_p[:, :TOP_K], idx_s[:, :TOP_K]], axis=0)
        gates = jnp.concatenate([gate_p[:, :TOP_K], gate_s[:, :TOP_K]], axis=0)
        n_tiles = (tp + ts) * TOP_K // tm_e + n_exp
        tile_expert, tile_first, row_block, n_used, row_token, row_gate, pos = _route(
            top_i, gates, n_exp, tm_e, n_tiles)
        x_rows = jnp.take(h2, row_token, axis=0)
        act = _moe_up(tile_expert, tile_first, row_block, n_used, x_rows, w_up[l], b_up[l], tm=tm_e, tf=tf_up)
        w_down_p = (w_down[l].reshape(n_exp, d_ff // LANES, 2, LANES // 2, d).transpose(0, 1, 3, 2, 4)
                    .reshape(n_exp, d_ff, d).astype(BF16))
        y_rows = _moe_down(tile_expert, row_block, n_used, act, w_down_p, b_down[l], row_gate, tm=tm_e, tn=tn_down)
        f = jnp.sum(jnp.take(y_rows, pos.reshape(-1), axis=0).reshape(tp + ts, TOP_K, d).astype(F32), axis=1)

        ln2 = (ln2_g[l].reshape(1, d), ln2_b[l].reshape(1, d))
        xp = _final_ln(x1_p, f[:tp], mod_p, spec_p, *ln2, tm=tm_p, alpha=alpha)
        xs = _final_ln(x1_s, f[tp:], mod_s, spec_s, *ln2, tm=tm_s, alpha=alpha)

    def stack_rows(rows, i, nb, t, h, dh):
        return jnp.stack([r[i].reshape(nb, t, h, dh) for r in rows], axis=2)

    return (xp.reshape(bp, seq, d), xs.reshape(bs, dec_seq, d),
            stack_rows(prompt_rows, 0, bp, seq, h_sb, d_sb), stack_rows(prompt_rows, 1, bp, seq, h_sb, d_sb),
            stack_rows(prompt_rows, 2, bp, seq, h_d, d_vd), stack_rows(prompt_rows, 3, bp, seq, h_d, d_vd),
            stack_rows(sample_rows, 0, bs, dec_seq, h_sb, d_sb), stack_rows(sample_rows, 1, bs, dec_seq, h_sb, d_sb),
            stack_rows(sample_rows, 2, bs, dec_seq, h_d, d_vd), stack_rows(sample_rows, 3, bs, dec_seq, h_d, d_vd))
```

```python
import functools
import math

import jax
import jax.numpy as jnp
from jax import lax
from jax.experimental import pallas as pl
from jax.experimental.pallas import tpu as pltpu

LN_EPS = 1e-5
TOP_K = 4
MAX_DISTANCE = 128
SWIGLU_LIMIT = 7.0
SWIGLU_ALPHA = 1.702
NEG_BIG = -1e30
LANES = 128
SUBLANES = 8
BF16_ROWS = 16
VMEM_LIMIT = 56 * 1024 * 1024

BF16 = jnp.bfloat16
F32 = jnp.float32
U32 = jnp.uint32


def _params(sem, vmem=VMEM_LIMIT):
    return pltpu.CompilerParams(dimension_semantics=sem, vmem_limit_bytes=vmem)


def _ln_rows(x):
    mu = jnp.mean(x, axis=-1, keepdims=True)
    xc = x - mu
    var = jnp.mean(xc * xc, axis=-1, keepdims=True)
    return xc * lax.rsqrt(var + LN_EPS)


def _dot(a, b):
    return jnp.dot(a, b, preferred_element_type=F32)


def _dot_t(a, b):
    return lax.dot_general(a, b, (((1,), (1,)), ((), ())), preferred_element_type=F32)


def _softplus(z):
    return jnp.maximum(z, 0.0) + jnp.log(1.0 + jnp.exp(-jnp.abs(z)))


def _split_bf16(x):
    hi = x.astype(BF16)
    lo = (x - hi.astype(F32)).astype(BF16)
    return hi, lo


def _sub_rms(o, g, post_scale):
    ms = jnp.mean(o * o, axis=-1, keepdims=True)
    return o * lax.rsqrt(ms + LN_EPS) * g * post_scale


def _pack_rows(y, dst_ref, n_rows):
    half = y.shape[1] // 2
    nsub = half // LANES
    for s in range(nsub):
        a = y[:, s * LANES:(s + 1) * LANES].astype(BF16).astype(F32)
        b = y[:, half + s * LANES:half + (s + 1) * LANES].astype(BF16).astype(F32)
        w = lax.bitcast_convert_type(a, U32) | (lax.bitcast_convert_type(b, U32) >> 16)
        dst_ref[pl.ds(s, n_rows, stride=nsub), :] = w


def _unpack_rows(src_ref, start, n_rows, nsub):
    hi, lo = [], []
    for s in range(nsub):
        w = src_ref[pl.ds(start + s, n_rows, stride=nsub), :]
        hi.append(lax.bitcast_convert_type(w & jnp.uint32(0xFFFF0000), F32))
        lo.append(lax.bitcast_convert_type(w << 16, F32))
    return hi, lo


def _ada_kernel(c_ref, w_ref, b_ref, o_ref):
    c = c_ref[...]
    s = (c * jax.nn.sigmoid(c)).astype(BF16)
    o_ref[...] = _dot(s, w_ref[...].astype(BF16)) + b_ref[...]


def _ada(c_all, w_ada, b_ada):
    rows, d = c_all.shape
    n = w_ada.shape[1]
    tn = min(n, 1024)
    return pl.pallas_call(
        _ada_kernel,
        grid=(n // tn,),
        in_specs=[pl.BlockSpec((rows, d), lambda j: (0, 0)),
                  pl.BlockSpec((d, tn), lambda j: (0, j)),
                  pl.BlockSpec((1, tn), lambda j: (0, j))],
        out_specs=pl.BlockSpec((rows, tn), lambda j: (0, j)),
        out_shape=jax.ShapeDtypeStruct((rows, n), F32),
        compiler_params=_params(("arbitrary",)),
        name="ada",
    )(c_all, w_ada, b_ada.reshape(1, n))


def _ln_proj_kernel(x_ref, sh_ref, sc_ref, w_ref, qsb_ref, ksb_ref, vsb_ref, qd_ref, kd_ref, vd_ref,
                    *, wsb, wd, sb_scale, d_scale):
    h = (_ln_rows(x_ref[...]) * (1.0 + sc_ref[...]) + sh_ref[...]).astype(BF16)

    def proj(lo, width):
        return _dot(h, w_ref[:, lo:lo + width])

    qsb_ref[...] = (proj(0, wsb) * sb_scale).astype(BF16)
    ksb_ref[...] = proj(wsb, wsb)
    vsb_ref[...] = proj(2 * wsb, wsb)
    qd_ref[...] = (proj(3 * wsb, wd) * d_scale).astype(BF16)
    kd_ref[...] = proj(3 * wsb + wd, wd)
    vd_ref[...] = proj(3 * wsb + 2 * wd, wd)


def _ln_proj(x, mod, mod_spec, w_in_bf, *, tm, wsb, wd, sb_scale, d_scale):
    t, d = x.shape
    kern = functools.partial(_ln_proj_kernel, wsb=wsb, wd=wd, sb_scale=sb_scale, d_scale=d_scale)
    row = lambda i: (i, 0)
    return pl.pallas_call(
        kern,
        grid=(t // tm,),
        in_specs=[pl.BlockSpec((tm, d), row), mod_spec(0), mod_spec(1),
                  pl.BlockSpec(w_in_bf.shape, lambda i: (0, 0), pipeline_mode=pl.Buffered(1))],
        out_specs=[pl.BlockSpec((tm, wsb), row)] * 3 + [pl.BlockSpec((tm, wd), row)] * 3,
        out_shape=[jax.ShapeDtypeStruct((t, wsb), BF16), jax.ShapeDtypeStruct((t, wsb), F32),
                   jax.ShapeDtypeStruct((t, wsb), F32), jax.ShapeDtypeStruct((t, wd), BF16),
                   jax.ShapeDtypeStruct((t, wd), F32), jax.ShapeDtypeStruct((t, wd), F32)],
        compiler_params=_params(("arbitrary",)),
        name="ln_proj",
    )(x, mod, mod, w_in_bf)


def _attn_prompt_kernel(far_ref, lam_ref,
                        qsb_ref, ksb_ref, vsb_ref, qd_ref, kd_ref, vd_ref, bias_ref, ut_ref, g_ref,
                        osb_ref, od_ref,
                        kbs_ref, vts_ref, kbd_ref, vtd_ref,
                        c_ref, accs_ref, m1_ref, l1_ref, a1_ref, m2_ref, l2_ref, a2_ref,
                        *, bq, nq, dqk, post_scale):
    h = pl.program_id(1)
    qi = pl.program_id(2)

    @pl.when(qi == 0)
    def _():
        for jb in range(nq):
            rs = slice(jb * bq, (jb + 1) * bq)
            kbs_ref[jb] = ksb_ref[rs, :].astype(BF16)
            kbd_ref[jb] = kd_ref[rs, :].astype(BF16)
            vts_ref[jb] = vsb_ref[rs, :].T.astype(BF16)
            vtd_ref[jb] = vd_ref[rs, :].T.astype(BF16)

    q_sb = qsb_ref[...]
    q_d = qd_ref[...]
    lane = lax.broadcasted_iota(jnp.int32, q_d.shape, 1)
    q1 = jnp.where(lane < dqk, q_d, jnp.zeros_like(q_d))
    q2 = jnp.where(lane >= dqk, q_d, jnp.zeros_like(q_d))
    ut = ut_ref[...]

    c_ref[...] = jnp.zeros_like(c_ref)
    accs_ref[...] = jnp.zeros_like(accs_ref)
    for m_ref, l_ref, a_ref in ((m1_ref, l1_ref, a1_ref), (m2_ref, l2_ref, a2_ref)):
        m_ref[...] = jnp.full_like(m_ref, NEG_BIG)
        l_ref[...] = jnp.zeros_like(l_ref)
        a_ref[...] = jnp.zeros_like(a_ref)

    def block(jb, bias, diag):
        if diag:
            s_i = lax.broadcasted_iota(jnp.int32, (bq, bq), 0)
            t_i = lax.broadcasted_iota(jnp.int32, (bq, bq), 1)
            strict, causal = s_i < t_i, s_i <= t_i
        z = _dot_t(kbs_ref[jb], q_sb)
        sp = _softplus(z)
        log_1m = -sp
        if diag:
            log_1m = jnp.where(strict, log_1m, 0.0)
        hi, lo = _split_bf16(log_1m)
        later = _dot(ut, hi) + _dot(ut, lo)
        a = jnp.exp((z - sp) + later + c_ref[...])
        if diag:
            a = jnp.where(strict, a, 0.0)
        accs_ref[...] += _dot(vts_ref[jb], a.astype(BF16))
        c_ref[...] += jnp.sum(log_1m, axis=0, keepdims=True)
        kd = kbd_ref[jb]
        vt = vtd_ref[jb]
        for qm, m_ref, l_ref, a_ref in ((q1, m1_ref, l1_ref, a1_ref), (q2, m2_ref, l2_ref, a2_ref)):
            sc = _dot_t(kd, qm) + bias
            if diag:
                sc = jnp.where(causal, sc, NEG_BIG)
            m_old = m_ref[...]
            m_new = jnp.maximum(m_old, jnp.max(sc, axis=0, keepdims=True))
            alpha = jnp.exp(m_old - m_new)
            p = jnp.exp(sc - m_new)
            l_ref[...] = alpha * l_ref[...] + jnp.sum(p, axis=0, keepdims=True)
            a_ref[...] = alpha * a_ref[...] + _dot(vt, p.astype(BF16))
            m_ref[...] = m_new

    block(qi, bias_ref[0], True)

    @pl.when(qi >= 1)
    def _():
        block(qi - 1, bias_ref[1], False)

    far = far_ref[h]
    n_far = jnp.maximum(qi - 1, 0)

    def far_pair(t, carry):
        block(qi - 2 - 2 * t, far, False)
        block(qi - 3 - 2 * t, far, False)
        return carry

    lax.fori_loop(0, n_far // 2, far_pair, 0)

    @pl.when(n_far % 2 == 1)
    def _():
        block(0, far, False)

    osb_ref[...] = accs_ref[...].T.astype(osb_ref.dtype)
    o_d = (a1_ref[...] / l1_ref[...] - lam_ref[0] * (a2_ref[...] / l2_ref[...])).T
    od_ref[...] = _sub_rms(o_d, g_ref[...], post_scale).astype(od_ref.dtype)


def _attn_prompt(q_sb, k_sb, v_sb, q_d, k_d, v_d, bias_tiles, far_bias, lam, ut_mat, subln_g,
                 *, batch, seq, heads, bq, post_scale):
    t, w = q_sb.shape
    dh = w // heads
    nq = seq // bq
    kern = functools.partial(_attn_prompt_kernel, bq=bq, nq=nq, dqk=dh // 2, post_scale=post_scale)
    qmap = lambda b, h, i, *_: (b * nq + i, h)
    kvmap = lambda b, h, i, *_: (b, h)
    row_stat = pltpu.VMEM((1, bq), F32)
    grid_spec = pltpu.PrefetchScalarGridSpec(
        num_scalar_prefetch=2,
        grid=(batch, heads, nq),
        in_specs=[pl.BlockSpec((bq, dh), qmap), pl.BlockSpec((seq, dh), kvmap), pl.BlockSpec((seq, dh), kvmap),
                  pl.BlockSpec((bq, dh), qmap), pl.BlockSpec((seq, dh), kvmap), pl.BlockSpec((seq, dh), kvmap),
                  pl.BlockSpec((None, 2, bq, bq), lambda b, h, i, *_: (h, 0, 0, 0)),
                  pl.BlockSpec((bq, bq), lambda b, h, i, *_: (0, 0)),
                  pl.BlockSpec((1, dh), lambda b, h, i, *_: (0, 0))],
        out_specs=[pl.BlockSpec((bq, dh), qmap), pl.BlockSpec((bq, dh), qmap)],
        scratch_shapes=[pltpu.VMEM((nq, bq, dh), BF16), pltpu.VMEM((nq, dh, bq), BF16),
                        pltpu.VMEM((nq, bq, dh), BF16), pltpu.VMEM((nq, dh, bq), BF16),
                        row_stat, pltpu.VMEM((dh, bq), F32),
                        row_stat, row_stat, pltpu.VMEM((dh, bq), F32),
                        row_stat, row_stat, pltpu.VMEM((dh, bq), F32)],
    )
    return pl.pallas_call(
        kern,
        grid_spec=grid_spec,
        out_shape=[jax.ShapeDtypeStruct((t, w), BF16), jax.ShapeDtypeStruct((t, w), BF16)],
        compiler_params=_params(("arbitrary", "arbitrary", "arbitrary")),
        name="attn_prompt",
    )(far_bias, lam, q_sb, k_sb, v_sb, q_d, k_d, v_d, bias_tiles, ut_mat, subln_g)


def _attn_sample_kernel(pt_ref, lam_ref,
                        qs_ref, qd_ref, kns_ref, vns_ref, knd_ref, vnd_ref,
                        bnew_ref, blast_ref, bfar_ref, u_ref, ones_ref, g_ref, *rest,
                        n_pg, heads, dec_seq, page, post_scale):
    pages = rest[:4 * n_pg]
    osb_ref, od_ref = rest[4 * n_pg:4 * n_pg + 2]
    c_ref, accs_ref, m_ref, l_ref, accd_ref = rest[4 * n_pg + 2:]
    ks_pages, vs_pages = pages[0:n_pg], pages[n_pg:2 * n_pg]
    kd_pages, vd_pages = pages[2 * n_pg:3 * n_pg], pages[3 * n_pg:4 * n_pg]
    j = pl.program_id(1)
    nj = pl.num_programs(1)
    rq = BF16_ROWS
    dh = g_ref.shape[1]
    u_mat = u_ref[...]
    ones = ones_ref[...]

    def qk_all(q_ref, keys):
        return _dot_t(q_ref[...], keys)

    def pv_all(p, values):
        full = _dot(p.astype(BF16), values)
        return jnp.concatenate([full[hh * rq:(hh + 1) * rq, hh * dh:(hh + 1) * dh] for hh in range(heads)], axis=0)

    def sb_keys(z, carry, strict):
        sp = _softplus(z)
        log_1m = -sp
        if strict is not None:
            log_1m = jnp.where(strict, log_1m, 0.0)
        hi, lo = _split_bf16(log_1m)
        later = _dot(hi, u_mat) + _dot(lo, u_mat)
        a = jnp.exp((z - sp) + later + carry)
        if strict is not None:
            a = jnp.where(strict, a, 0.0)
        return a, carry + (_dot(hi, ones) + _dot(lo, ones))

    def softmax_update(s_list, v_list, m_old, l_old, acc_old):
        mx = functools.reduce(jnp.maximum, [jnp.max(s, axis=1, keepdims=True) for s in s_list])
        m_new = jnp.maximum(m_old, mx)
        alpha = jnp.exp(m_old - m_new)
        l_new = alpha * l_old
        acc = alpha * acc_old
        for s, v in zip(s_list, v_list):
            p = jnp.exp(s - m_new)
            l_new = l_new + jnp.sum(p, axis=1, keepdims=True)
            acc = acc + pv_all(p, v)
        return m_new, l_new, acc

    def token_major(ref):
        return jnp.concatenate([ref[pl.ds(hh, page, stride=heads), :] for hh in range(heads)], axis=1).astype(BF16)

    @pl.when(j == 0)
    def _():
        shape = (heads * rq, kns_ref.shape[0])
        r_i = lax.broadcasted_iota(jnp.int32, shape, 0)
        s_i = lax.broadcasted_iota(jnp.int32, shape, 1)
        z = qk_all(qs_ref, kns_ref[...])
        a, carry = sb_keys(z, jnp.zeros(shape, F32), s_i < r_i % rq)
        accs_ref[...] = pv_all(a, vns_ref[...])
        c_ref[...] = carry
        sc = qk_all(qd_ref, knd_ref[...]) + bnew_ref[...]
        sc = jnp.where(s_i <= r_i % dec_seq, sc, NEG_BIG)
        m_new, l_new, acc = softmax_update(
            [sc], [vnd_ref[...]], jnp.full((shape[0], 1), NEG_BIG, F32),
            jnp.zeros((shape[0], 1), F32), jnp.zeros(accd_ref.shape, F32))
        m_ref[...] = m_new
        l_ref[...] = l_new
        accd_ref[...] = acc

    carry = c_ref[...]
    acc_s = accs_ref[...]
    for p in range(n_pg):
        z = qk_all(qs_ref, token_major(ks_pages[p]))
        a, carry = sb_keys(z, carry, None)
        acc_s = acc_s + pv_all(a, token_major(vs_pages[p]))
    accs_ref[...] = acc_s
    c_ref[...] = carry

    s_list, v_list = [], []
    for p in range(n_pg):
        bias = bfar_ref[...]
        if p == 0:
            bias = jnp.where(j == 0, blast_ref[...], bias)
        s_list.append(qk_all(qd_ref, token_major(kd_pages[p])) + bias)
        v_list.append(token_major(vd_pages[p]))
    m_new, l_new, acc = softmax_update(s_list, v_list, m_ref[...], l_ref[...], accd_ref[...])
    m_ref[...] = m_new
    l_ref[...] = l_new
    accd_ref[...] = acc

    @pl.when(j == nj - 1)
    def _():
        acc_s = accs_ref[...]
        dn = accd_ref[...] / l_ref[...]
        g = g_ref[...]
        for hh in range(heads):
            cs = slice(hh * dh, (hh + 1) * dh)
            osb_ref[:, cs] = acc_s[hh * rq:hh * rq + dec_seq].astype(osb_ref.dtype)
            o_d = dn[hh * rq:hh * rq + dec_seq] - lam_ref[0] * dn[hh * rq + dec_seq:hh * rq + 2 * dec_seq]
            od_ref[:, cs] = _sub_rms(o_d, g, post_scale).astype(od_ref.dtype)


def _attn_sample(page_table, lam, q_sb16, q_d16, new_kv, caches, bias_new, bias_last, bias_far, u_mat, ones_mat,
                 subln_g, *, layer, depth, heads, dec_seq, page, n_pg, post_scale):
    nb, n_pages = page_table.shape
    rows, w = q_sb16.shape[1:]
    dh = w // heads
    nj = n_pages // n_pg
    kern = functools.partial(_attn_sample_kernel, n_pg=n_pg, heads=heads, dec_seq=dec_seq, page=page,
                             post_scale=post_scale)
    bmap = lambda b, j, *_: (b, 0, 0)
    const = lambda b, j, *_: (0, 0)

    def page_spec(p):
        def imap(b, j, pt, lam_):
            return (pt[b * n_pages + (n_pages - 1 - (j * n_pg + p))] * depth + layer, 0, 0)
        return pl.BlockSpec((None, page * heads, dh), imap)

    in_specs = [pl.BlockSpec((None, rows, w), bmap)] * 2
    in_specs += [pl.BlockSpec((None, page, w), bmap)] * 4
    in_specs += [pl.BlockSpec((rows, page), const)] * 3 + [pl.BlockSpec((page, page), const)] * 2
    in_specs += [pl.BlockSpec((1, dh), const)]
    for _ in range(4):
        in_specs += [page_spec(p) for p in range(n_pg)]
    omap = lambda b, j, *_: (b, 0)
    grid_spec = pltpu.PrefetchScalarGridSpec(
        num_scalar_prefetch=2,
        grid=(nb, nj),
        in_specs=in_specs,
        out_specs=[pl.BlockSpec((dec_seq, w), omap), pl.BlockSpec((dec_seq, w), omap)],
        scratch_shapes=[pltpu.VMEM((rows, page), F32), pltpu.VMEM((rows, dh), F32),
                        pltpu.VMEM((rows, 1), F32), pltpu.VMEM((rows, 1), F32), pltpu.VMEM((rows, dh), F32)],
    )
    page_args = []
    for c in caches:
        page_args += [c] * n_pg
    return pl.pallas_call(
        kern,
        grid_spec=grid_spec,
        out_shape=[jax.ShapeDtypeStruct((nb * dec_seq, w), F32)] * 2,
        compiler_params=_params(("arbitrary", "arbitrary")),
        name="attn_sample",
    )(page_table.reshape(-1), lam, q_sb16, q_d16, *new_kv, bias_new, bias_last, bias_far, u_mat, ones_mat,
      subln_g, *page_args)


def _out_router_kernel(osb_ref, od_ref, x_ref, g1_ref, sh2_ref, sc2_ref, wo_ref, lng_ref, lnb_ref,
                       wrh_ref, wrl_ref, br_ref, x1_ref, h2_ref, idx_ref, gate_ref, *, wsb, alpha):
    o = (_dot(osb_ref[...].astype(BF16), wo_ref[:wsb, :]) + _dot(od_ref[...].astype(BF16), wo_ref[wsb:, :]))
    x1 = _ln_rows(alpha * x_ref[...] + g1_ref[...] * o) * lng_ref[...] + lnb_ref[...]
    x1_ref[...] = x1
    h2 = _ln_rows(x1) * (1.0 + sc2_ref[...]) + sh2_ref[...]
    _pack_rows(h2, h2_ref, h2.shape[0])
    hi, lo = _split_bf16(h2)
    logits = (_dot(hi, wrh_ref[...]) + _dot(lo, wrh_ref[...]) + _dot(hi, wrl_ref[...])) + br_ref[...]
    lane = lax.broadcasted_iota(jnp.int32, logits.shape, 1)
    lane_f = lane.astype(F32)
    vals, idxs = [], []
    for _ in range(TOP_K):
        mx = jnp.max(logits, axis=1, keepdims=True)
        ix = jnp.min(jnp.where(logits == mx, lane_f, float(logits.shape[1])), axis=1, keepdims=True)
        vals.append(mx)
        idxs.append(ix.astype(jnp.int32))
        logits = jnp.where(lane_f == ix, -jnp.inf, logits)
    exps = [jnp.exp(v - vals[0]) for v in vals]
    denom = functools.reduce(lambda a, b: a + b, exps)
    idx_out = jnp.zeros(lane.shape, jnp.int32)
    gate_out = jnp.zeros(lane.shape, F32)
    for k in range(TOP_K):
        idx_out = jnp.where(lane == k, idxs[k], idx_out)
        gate_out = jnp.where(lane == k, exps[k] / denom, gate_out)
    idx_ref[...] = idx_out
    gate_ref[...] = gate_out


def _out_router(o_sb, o_d, x, mod, mod_spec, w_out_bf, ln_g, ln_b, wr_hi, wr_lo, br_pad, *, tm, alpha):
    t, d = x.shape
    wsb = o_sb.shape[1]
    ne = wr_hi.shape[1]
    nsub = d // (2 * LANES)
    row = lambda i: (i, 0)
    const = lambda i: (0, 0)
    kern = functools.partial(_out_router_kernel, wsb=wsb, alpha=alpha)
    return pl.pallas_call(
        kern,
        grid=(t // tm,),
        in_specs=[pl.BlockSpec((tm, wsb), row), pl.BlockSpec((tm, o_d.shape[1]), row), pl.BlockSpec((tm, d), row),
                  mod_spec(2), mod_spec(3), mod_spec(4),
                  pl.BlockSpec(w_out_bf.shape, const, pipeline_mode=pl.Buffered(1)),
                  pl.BlockSpec((1, d), const), pl.BlockSpec((1, d), const),
                  pl.BlockSpec((d, ne), const), pl.BlockSpec((d, ne), const), pl.BlockSpec((1, ne), const)],
        out_specs=[pl.BlockSpec((tm, d), row), pl.BlockSpec((tm * nsub, LANES), row),
                   pl.BlockSpec((tm, ne), row), pl.BlockSpec((tm, ne), row)],
        out_shape=[jax.ShapeDtypeStruct((t, d), F32), jax.ShapeDtypeStruct((t * nsub, LANES), U32),
                   jax.ShapeDtypeStruct((t, ne), jnp.int32), jax.ShapeDtypeStruct((t, ne), F32)],
        compiler_params=_params(("arbitrary",)),
        name="out_router",
    )(o_sb, o_d, x, mod, mod, mod, w_out_bf, ln_g, ln_b, wr_hi, wr_lo, br_pad)


def _row_copy(src_ref, src_row, dst_ref, dst_row, nsub, sem):
    def rows(r):
        return pl.ds(r * nsub if isinstance(r, int) else pl.multiple_of(r * nsub, nsub), nsub)

    return pltpu.make_async_copy(src_ref.at[rows(src_row)], dst_ref.at[rows(dst_row)], sem)


def _dispatch_kernel(rowp_ref, src_ref, init_ref, dst_ref, sem, *, chunk, top_k, nsub):
    del init_ref
    tok0 = pl.program_id(0) * (chunk // top_k)

    def start(t, carry):
        for k in range(top_k):
            _row_copy(src_ref, tok0 + t, dst_ref, rowp_ref[t * top_k + k], nsub, sem).start()
        return carry

    lax.fori_loop(0, chunk // top_k, start, 0)
    pltpu.make_async_copy(src_ref.at[pl.ds(0, chunk * nsub)], dst_ref.at[pl.ds(0, chunk * nsub)], sem).wait()


def _dispatch(row_p, h2_packed, n_rows, *, chunk, top_k, nsub):
    n_pairs = row_p.shape[0]
    init = jnp.zeros((n_rows * nsub, LANES), U32)
    kern = functools.partial(_dispatch_kernel, chunk=chunk, top_k=top_k, nsub=nsub)
    return pl.pallas_call(
        kern,
        grid=(n_pairs // chunk,),
        in_specs=[pl.BlockSpec((chunk,), lambda i: (i,), memory_space=pltpu.SMEM),
                  pl.BlockSpec(memory_space=pl.ANY), pl.BlockSpec(memory_space=pl.ANY)],
        out_specs=pl.BlockSpec(memory_space=pl.ANY),
        out_shape=jax.ShapeDtypeStruct(init.shape, U32),
        scratch_shapes=[pltpu.SemaphoreType.DMA],
        input_output_aliases={2: 0},
        compiler_params=_params(("arbitrary",)),
        name="dispatch",
    )(row_p, h2_packed, init)


def _moe_up_kernel(te_ref, tf_ref, rb_ref, nu_ref, x_ref, w_ref, b_ref, perm_ref, o_ref, wb_ref, *, tm, nsub):
    i = pl.program_id(1)

    @pl.when(tf_ref[i] == 1)
    def _():
        wb_ref[...] = w_ref[...].astype(BF16)

    @pl.when(i < nu_ref[0])
    def _():
        hi, lo = _unpack_rows(x_ref, 0, tm, nsub)
        x = jnp.concatenate(hi + lo, axis=1).astype(BF16)
        u = _dot(x, wb_ref[...]) + b_ref[...]
        even = (lax.broadcasted_iota(jnp.int32, (tm, LANES), 1) & 1) == 0
        perm = perm_ref[...]
        for c in range(u.shape[1] // (2 * LANES)):
            a = u[:, 2 * c * LANES:(2 * c + 1) * LANES]
            b = u[:, (2 * c + 1) * LANES:(2 * c + 2) * LANES]
            glu = jnp.where(even, a, pltpu.roll(b, 1, 1))
            lin = jnp.where(even, pltpu.roll(a, LANES - 1, 1), b)
            glu = jnp.minimum(glu, SWIGLU_LIMIT)
            lin = jnp.clip(lin, -SWIGLU_LIMIT, SWIGLU_LIMIT)
            act = (glu * jax.nn.sigmoid(SWIGLU_ALPHA * glu) * (lin + 1.0)).astype(BF16)
            o_ref[:, c * LANES:(c + 1) * LANES] = _dot(act, perm).astype(o_ref.dtype)


def _moe_up(tile_expert, tile_first, row_block, n_used, xs, w_up, b_up, perm, *, tm, tf, nsub):
    r_pad = xs.shape[0] // nsub
    ne, d, f2 = w_up.shape
    f = f2 // 2
    n_tiles = r_pad // tm
    kern = functools.partial(_moe_up_kernel, tm=tm, nsub=nsub)
    grid_spec = pltpu.PrefetchScalarGridSpec(
        num_scalar_prefetch=4,
        grid=(f // tf, n_tiles),
        in_specs=[pl.BlockSpec((tm * nsub, LANES), lambda j, i, te, tfr, rb, nu: (rb[i], 0)),
                  pl.BlockSpec((None, d, 2 * tf), lambda j, i, te, tfr, rb, nu: (te[i], 0, j)),
                  pl.BlockSpec((None, 1, 2 * tf), lambda j, i, te, tfr, rb, nu: (te[i], 0, j)),
                  pl.BlockSpec((LANES, LANES), lambda j, i, te, tfr, rb, nu: (0, 0))],
        out_specs=pl.BlockSpec((tm, tf), lambda j, i, te, tfr, rb, nu: (rb[i], j)),
        scratch_shapes=[pltpu.VMEM((d, 2 * tf), BF16)],
    )
    return pl.pallas_call(
        kern,
        grid_spec=grid_spec,
        out_shape=jax.ShapeDtypeStruct((r_pad, f), BF16),
        compiler_params=_params(("arbitrary", "arbitrary")),
        name="moe_up",
    )(tile_expert, tile_first, row_block, n_used, xs, w_up, b_up.reshape(ne, 1, f2), perm)


def _moe_down_kernel(te_ref, tf_ref, rb_ref, nu_ref, a_ref, w_ref, b_ref, o_ref, wb_ref, *, tm):
    i = pl.program_id(0)

    @pl.when(tf_ref[i] == 1)
    def _():
        wb_ref[...] = w_ref[...].astype(BF16)

    @pl.when(i < nu_ref[0])
    def _():
        _pack_rows(_dot(a_ref[...], wb_ref[...]) + b_ref[...], o_ref, tm)


def _moe_down(tile_expert, tile_first, row_block, n_used, act, w_down, b_down, *, tm, nsub):
    r_pad, f = act.shape
    ne, _, d = w_down.shape
    n_tiles = r_pad // tm
    grid_spec = pltpu.PrefetchScalarGridSpec(
        num_scalar_prefetch=4,
        grid=(n_tiles,),
        in_specs=[pl.BlockSpec((tm, f), lambda i, te, tfr, rb, nu: (rb[i], 0)),
                  pl.BlockSpec((None, f, d), lambda i, te, tfr, rb, nu: (te[i], 0, 0)),
                  pl.BlockSpec((None, 1, d), lambda i, te, tfr, rb, nu: (te[i], 0, 0))],
        out_specs=pl.BlockSpec((tm * nsub, LANES), lambda i, te, tfr, rb, nu: (rb[i], 0)),
        scratch_shapes=[pltpu.VMEM((f, d), BF16)],
    )
    return pl.pallas_call(
        functools.partial(_moe_down_kernel, tm=tm),
        grid_spec=grid_spec,
        out_shape=jax.ShapeDtypeStruct((r_pad * nsub, LANES), U32),
        compiler_params=_params(("arbitrary",)),
        name="moe_down",
    )(tile_expert, tile_first, row_block, n_used, act, w_down, b_down.reshape(ne, 1, d))


def _route(top_i, n_exp, tm, n_tiles):
    flat_e = top_i.reshape(-1)
    onehot = (flat_e[:, None] == jnp.arange(n_exp, dtype=jnp.int32)[None, :]).astype(jnp.int32)
    csum = jnp.cumsum(onehot, axis=0)
    rank = jnp.sum((csum - onehot) * onehot, axis=1)
    counts = csum[-1]
    tiles_e = (counts + tm - 1) // tm
    tile_end = jnp.cumsum(tiles_e)
    tile_start = tile_end - tiles_e
    n_used = tile_end[-1]
    row_p = (jnp.sum(onehot * tile_start[None, :], axis=1) * tm + rank).astype(jnp.int32)
    tile_ids = jnp.arange(n_tiles, dtype=jnp.int32)
    clamped = jnp.minimum(tile_ids, n_used - 1)
    tile_expert = jnp.minimum(jnp.sum((tile_end[None, :] <= clamped[:, None]).astype(jnp.int32), axis=1), n_exp - 1)
    tile_first = (jnp.any(tile_ids[:, None] == tile_start[None, :], axis=1) & (tile_ids < n_used)).astype(jnp.int32)
    return (tile_expert.astype(jnp.int32), tile_first, clamped.astype(jnp.int32),
            n_used.reshape(1).astype(jnp.int32), row_p)


def _combine_kernel(pos_ref, x1_ref, gate_ref, g2_ref, lng_ref, lnb_ref, y_ref, o_ref, buf_ref, sem,
                    *, tm, top_k, nsub, alpha):
    def start(t, carry):
        for k in range(top_k):
            _row_copy(y_ref, pos_ref[t * top_k + k], buf_ref, k * tm + t, nsub, sem).start()
        return carry

    lax.fori_loop(0, tm, start, 0)
    pltpu.make_async_copy(y_ref.at[pl.ds(0, tm * top_k * nsub)], buf_ref, sem).wait()

    gates = gate_ref[...]
    f_hi = [jnp.zeros((tm, LANES), F32)] * nsub
    f_lo = [jnp.zeros((tm, LANES), F32)] * nsub
    for k in range(top_k):
        gk = gates[:, k:k + 1]
        hi, lo = _unpack_rows(buf_ref, k * tm * nsub, tm, nsub)
        f_hi = [acc + gk * v for acc, v in zip(f_hi, hi)]
        f_lo = [acc + gk * v for acc, v in zip(f_lo, lo)]
    f = jnp.concatenate(f_hi + f_lo, axis=1)
    y = alpha * x1_ref[...] + g2_ref[...] * f
    o_ref[...] = _ln_rows(y) * lng_ref[...] + lnb_ref[...]


def _combine_ln(pos, tile_offset, x1, gates, mod, mod_spec, ln_g, ln_b, y_rows, *, tm, top_k, nsub, alpha):
    t, d = x1.shape
    ne = gates.shape[1]
    row = lambda i: (i, 0)
    const = lambda i: (0, 0)
    kern = functools.partial(_combine_kernel, tm=tm, top_k=top_k, nsub=nsub, alpha=alpha)
    return pl.pallas_call(
        kern,
        grid=(t // tm,),
        in_specs=[pl.BlockSpec((tm * top_k,), lambda i: (i + tile_offset,), memory_space=pltpu.SMEM),
                  pl.BlockSpec((tm, d), row), pl.BlockSpec((tm, ne), lambda i: (i + tile_offset, 0)), mod_spec(5),
                  pl.BlockSpec((1, d), const), pl.BlockSpec((1, d), const),
                  pl.BlockSpec(memory_space=pl.ANY)],
        out_specs=pl.BlockSpec((tm, d), row),
        out_shape=jax.ShapeDtypeStruct((t, d), F32),
        scratch_shapes=[pltpu.VMEM((tm * top_k * nsub, LANES), U32), pltpu.SemaphoreType.DMA],
        compiler_params=_params(("arbitrary",)),
        name="combine_ln",
    )(pos, x1, gates, mod, ln_g, ln_b, y_rows)


def _t5_bucket(rel, n_buckets):
    n = jnp.maximum(rel, 0)
    max_exact = n_buckets // 2
    nf = jnp.maximum(n, 1).astype(F32)
    large = max_exact + (jnp.log(nf / max_exact) / math.log(MAX_DISTANCE / max_exact)
                         * (n_buckets - max_exact)).astype(jnp.int32)
    large = jnp.minimum(large, n_buckets - 1)
    return jnp.where(n < max_exact, n, large)


def _bias_tile(rel, rel_bias):
    n_buckets = rel_bias.shape[0]
    onehot = (_t5_bucket(rel, n_buckets)[..., None] == jnp.arange(n_buckets)).astype(F32)
    return jnp.einsum("rsk,kh->hrs", onehot, rel_bias.astype(F32), precision=lax.Precision.HIGHEST)


def _sample_queries(q, nb, dec_seq, heads, maps):
    dh = q.shape[1] // heads
    q4 = q.reshape(nb, dec_seq, heads, dh).transpose(0, 2, 1, 3)
    lane = jnp.arange(dh)
    parts = [jnp.where((lane >= lo) & (lane < hi), q4, jnp.zeros_like(q4)) for lo, hi in maps]
    q16 = jnp.concatenate(parts, axis=2)
    q16 = jnp.pad(q16, ((0, 0), (0, 0), (0, BF16_ROWS - q16.shape[2]), (0, 0)))
    eye = jnp.eye(heads, dtype=q.dtype)
    out = q16[:, :, :, None, :] * eye[None, :, None, :, None]
    return out.reshape(nb, heads * BF16_ROWS, heads * dh)


def kernel(x_prompt, x_sample, cache_k_sb, cache_v_sb, cache_k_diff, cache_v_diff, page_table, c_prompt, c_sample,
           w_ada, b_ada, w_in, w_out, rel_bias, lambda_q1, lambda_k1, lambda_q2, lambda_k2, subln_g, ln1_g, ln1_b,
           w_router, b_router, w_up, b_up, w_down, b_down, ln2_g, ln2_b):
    bp, seq, d = x_prompt.shape
    bs, dec_seq, _ = x_sample.shape
    n_pool, page, depth, h_sb, d_sb = cache_k_sb.shape
    _, _, _, h_d, d_vd = cache_k_diff.shape
    n_pages = page_table.shape[1]
    wsb, wd = h_sb * d_sb, h_d * d_vd
    n_exp = w_router.shape[-1]
    n_buckets = rel_bias.shape[0]
    d_ff = w_down.shape[2]
    alpha = (2.0 * depth) ** 0.25
    tp, ts = bp * seq, bs * dec_seq
    assert h_sb == h_d and d_sb == d_vd == LANES and page >= MAX_DISTANCE and 2 * dec_seq == BF16_ROWS
    heads = h_sb
    nsub = d // (2 * LANES)

    tm_p = min(256, seq)
    tm_s = ts
    bq = min(256, seq)
    n_pg = min(8, n_pages)
    tm_e = min(256, ts)
    tf_up = min(512, d_ff)
    chunk = math.gcd((tp + ts) * TOP_K, 1024)
    assert seq % bq == 0 and bq >= MAX_DISTANCE and n_pages % n_pg == 0
    assert tp % tm_e == 0 and ts % tm_e == 0

    xp = x_prompt.reshape(tp, d)
    xs = x_sample.reshape(ts, d)
    c_all = jnp.concatenate([c_prompt, c_sample, jnp.zeros(((-(bp + bs)) % BF16_ROWS, d), F32)], axis=0)

    caches = [jnp.swapaxes(c, 1, 2).reshape(n_pool * depth, page * heads, d_sb) if depth > 1
              else c.reshape(n_pool, page * heads, d_sb)
              for c in (cache_k_sb, cache_v_sb, cache_k_diff, cache_v_diff)]

    ar = jnp.arange
    rel_t = ar(bq)[None, :] - ar(bq)[:, None]
    bias_prompt = jnp.stack([_bias_tile(rel_t, rel_bias), _bias_tile(rel_t + bq, rel_bias)], axis=1)
    far_bias = rel_bias[n_buckets - 1].astype(F32)
    rows16 = heads * BF16_ROWS
    t_rows = jnp.tile(ar(dec_seq), rows16 // dec_seq)
    h_rows = jnp.repeat(ar(heads), BF16_ROWS)
    pick = lambda tile: tile[h_rows, ar(rows16)]
    bias_new = pick(_bias_tile(t_rows[:, None] - ar(page)[None, :], rel_bias))
    bias_last = pick(_bias_tile(page + t_rows[:, None] - ar(page)[None, :], rel_bias))
    bias_far = jnp.broadcast_to(far_bias[h_rows][:, None], (rows16, page))
    later = lambda n: (ar(n)[:, None] > ar(n)[None, :]).astype(BF16)
    ut_blk = later(bq).T
    u_page = later(page)
    ones_page = jnp.ones((page, page), BF16)
    lane = ar(LANES)
    perm = (lane[None, :] == (lane[:, None] // 2 + (LANES // 2) * (lane[:, None] % 2))).astype(BF16)

    w_in_bf = w_in.astype(BF16)
    w_out_bf = w_out.astype(BF16)

    prompt_rows, sample_rows = [], []
    for l in range(depth):
        ada = _ada(c_all, w_ada[l], b_ada[l])
        mod_p = ada[:bp].reshape(bp, 1, 6 * d)
        mod_s = jnp.repeat(ada[bp:bp + bs], dec_seq, axis=0).reshape(1, ts, 6 * d)
        tiles_per_b = seq // tm_p
        spec_p = lambda k: pl.BlockSpec((None, 1, d), lambda i, k=k: (i // tiles_per_b, 0, k))
        spec_s = lambda k: pl.BlockSpec((None, tm_s, d), lambda i, k=k: (0, i, k))

        lam_init = 0.8 - 0.6 * math.exp(-0.3 * l)
        lam = (jnp.exp(jnp.sum(lambda_q1[l].astype(F32) * lambda_k1[l].astype(F32)))
               - jnp.exp(jnp.sum(lambda_q2[l].astype(F32) * lambda_k2[l].astype(F32))) + lam_init).reshape(1)
        g_sub = subln_g[l].reshape(1, d_vd).astype(F32)
        post = 1.0 - lam_init

        proj_kw = dict(wsb=wsb, wd=wd, sb_scale=d_sb ** -0.5, d_scale=(d_vd // 2) ** -0.5)
        qsb_p, ksb_p, vsb_p, qd_p, kd_p, vd_p = _ln_proj(xp, mod_p, spec_p, w_in_bf[l], tm=tm_p, **proj_kw)
        qsb_s, ksb_s, vsb_s, qd_s, kd_s, vd_s = _ln_proj(xs, mod_s, spec_s, w_in_bf[l], tm=tm_s, **proj_kw)
        prompt_rows.append((ksb_p, vsb_p, kd_p, vd_p))
        sample_rows.append((ksb_s, vsb_s, kd_s, vd_s))

        osb_p, od_p = _attn_prompt(qsb_p, ksb_p, vsb_p, qd_p, kd_p, vd_p, bias_prompt, far_bias, lam, ut_blk, g_sub,
                                   batch=bp, seq=seq, heads=heads, bq=bq, post_scale=post)

        pad_new = lambda a: jnp.pad(a.reshape(bs, dec_seq, -1).astype(BF16), ((0, 0), (0, page - dec_seq), (0, 0)))
        q_sb16 = _sample_queries(qsb_s, bs, dec_seq, heads, [(0, d_sb)])
        q_d16 = _sample_queries(qd_s, bs, dec_seq, heads, [(0, d_vd // 2), (d_vd // 2, d_vd)])
        osb_s, od_s = _attn_sample(page_table, lam, q_sb16, q_d16,
                                   [pad_new(a) for a in (ksb_s, vsb_s, kd_s, vd_s)], caches,
                                   bias_new, bias_last, bias_far, u_page, ones_page, g_sub,
                                   layer=l, depth=depth, heads=heads, dec_seq=dec_seq, page=page, n_pg=n_pg,
                                   post_scale=post)

        ne_pad = LANES
        wr = jnp.pad(w_router[l].astype(F32), ((0, 0), (0, ne_pad - n_exp)))
        wr_hi = wr.astype(BF16)
        wr_lo = (wr - wr_hi.astype(F32)).astype(BF16)
        br_pad = jnp.pad(b_router[l].astype(F32), (0, ne_pad - n_exp), constant_values=NEG_BIG).reshape(1, ne_pad)
        ln1 = (ln1_g[l].reshape(1, d), ln1_b[l].reshape(1, d))
        x1_p, h2_p, idx_p, gate_p = _out_router(osb_p, od_p, xp, mod_p, spec_p, w_out_bf[l], *ln1,
                                                wr_hi, wr_lo, br_pad, tm=tm_p, alpha=alpha)
        x1_s, h2_s, idx_s, gate_s = _out_router(osb_s, od_s, xs, mod_s, spec_s, w_out_bf[l], *ln1,
                                                wr_hi, wr_lo, br_pad, tm=tm_s, alpha=alpha)

        h2 = jnp.concatenate([h2_p, h2_s], axis=0)
        top_i = jnp.concatenate([idx_p[:, :TOP_K], idx_s[:, :TOP_K]], axis=0)
        gates = jnp.concatenate([gate_p, gate_s], axis=0)
        n_tiles = (tp + ts) * TOP_K // tm_e + n_exp
        tile_expert, tile_first, row_block, n_used, row_p = _route(top_i, n_exp, tm_e, n_tiles)
        x_rows = _dispatch(row_p, h2, n_tiles * tm_e, chunk=chunk, top_k=TOP_K, nsub=nsub)
        act = _moe_up(tile_expert, tile_first, row_block, n_used, x_rows, w_up[l], b_up[l], perm,
                      tm=tm_e, tf=tf_up, nsub=nsub)
        y_rows = _moe_down(tile_expert, tile_first, row_block, n_used, act, w_down[l], b_down[l], tm=tm_e, nsub=nsub)

        ln2 = (ln2_g[l].reshape(1, d), ln2_b[l].reshape(1, d))
        comb_kw = dict(tm=tm_e, top_k=TOP_K, nsub=nsub, alpha=alpha)
        tiles_per_b_e = seq // tm_e
        spec_pe = lambda k: pl.BlockSpec((None, 1, d), lambda i, k=k: (i // tiles_per_b_e, 0, k))
        spec_se = lambda k: pl.BlockSpec((None, tm_e, d), lambda i, k=k: (0, i, k))
        xp = _combine_ln(row_p, 0, x1_p, gates, mod_p, spec_pe, *ln2, y_rows, **comb_kw)
        xs = _combine_ln(row_p, tp // tm_e, x1_s, gates, mod_s, spec_se, *ln2, y_rows, **comb_kw)

    def stack_rows(rows, i, nb, t, h, dh):
        return jnp.stack([r[i].reshape(nb, t, h, dh) for r in rows], axis=2)

    return (xp.reshape(bp, seq, d), xs.reshape(bs, dec_seq, d),
            stack_rows(prompt_rows, 0, bp, seq, h_sb, d_sb), stack_rows(prompt_rows, 1, bp, seq, h_sb, d_sb),
            stack_rows(prompt_rows, 2, bp, seq, h_d, d_vd), stack_rows(prompt_rows, 3, bp, seq, h_d, d_vd),
            stack_rows(sample_rows, 0, bs, dec_seq, h_sb, d_sb), stack_rows(sample_rows, 1, bs, dec_seq, h_sb, d_sb),
            stack_rows(sample_rows, 2, bs, dec_seq, h_d, d_vd), stack_rows(sample_rows, 3, bs, dec_seq, h_d, d_vd))
```

```python
import functools
import math

import jax
import jax.numpy as jnp
from jax import lax
from jax.experimental import pallas as pl
from jax.experimental.pallas import tpu as pltpu

LN_EPS = 1e-5
TOP_K = 4
MAX_DISTANCE = 128
SWIGLU_LIMIT = 7.0
SWIGLU_ALPHA = 1.702
NEG_BIG = -1e30
LANES = 128
SUBLANES = 8
BF16_ROWS = 16
VMEM_LIMIT = 56 * 1024 * 1024

BF16 = jnp.bfloat16
F32 = jnp.float32
U32 = jnp.uint32


def _params(sem, vmem=VMEM_LIMIT):
    return pltpu.CompilerParams(dimension_semantics=sem, vmem_limit_bytes=vmem)


def _ln_rows(x):
    mu = jnp.mean(x, axis=-1, keepdims=True)
    xc = x - mu
    var = jnp.mean(xc * xc, axis=-1, keepdims=True)
    return xc * lax.rsqrt(var + LN_EPS)


def _dot(a, b):
    return jnp.dot(a, b, preferred_element_type=F32)


def _dot_t(a, b):
    return lax.dot_general(a, b, (((1,), (1,)), ((), ())), preferred_element_type=F32)


def _softplus(z):
    return jnp.maximum(z, 0.0) + jnp.log(1.0 + jnp.exp(-jnp.abs(z)))


def _split_bf16(x):
    hi = x.astype(BF16)
    lo = (x - hi.astype(F32)).astype(BF16)
    return hi, lo


def _sub_rms(o, g, post_scale):
    ms = jnp.mean(o * o, axis=-1, keepdims=True)
    return o * lax.rsqrt(ms + LN_EPS) * g * post_scale


def _pack_rows(y, dst_ref, n_rows):
    half = y.shape[1] // 2
    nsub = half // LANES
    for s in range(nsub):
        a = y[:, s * LANES:(s + 1) * LANES].astype(BF16).astype(F32)
        b = y[:, half + s * LANES:half + (s + 1) * LANES].astype(BF16).astype(F32)
        w = lax.bitcast_convert_type(a, U32) | (lax.bitcast_convert_type(b, U32) >> 16)
        dst_ref[pl.ds(s, n_rows, stride=nsub), :] = w


def _unpack_rows(src_ref, start, n_rows, nsub):
    hi, lo = [], []
    for s in range(nsub):
        w = src_ref[pl.ds(start + s, n_rows, stride=nsub), :]
        hi.append(lax.bitcast_convert_type(w & jnp.uint32(0xFFFF0000), F32))
        lo.append(lax.bitcast_convert_type(w << 16, F32))
    return hi, lo


def _ada_kernel(c_ref, w_ref, b_ref, o_ref):
    c = c_ref[...]
    s = (c * jax.nn.sigmoid(c)).astype(BF16)
    o_ref[...] = _dot(s, w_ref[...].astype(BF16)) + b_ref[...]


def _ada(c_all, w_ada, b_ada):
    rows, d = c_all.shape
    n = w_ada.shape[1]
    tn = min(n, 1024)
    return pl.pallas_call(
        _ada_kernel,
        grid=(n // tn,),
        in_specs=[pl.BlockSpec((rows, d), lambda j: (0, 0)),
                  pl.BlockSpec((d, tn), lambda j: (0, j)),
                  pl.BlockSpec((1, tn), lambda j: (0, j))],
        out_specs=pl.BlockSpec((rows, tn), lambda j: (0, j)),
        out_shape=jax.ShapeDtypeStruct((rows, n), F32),
        compiler_params=_params(("arbitrary",)),
        name="ada",
    )(c_all, w_ada, b_ada.reshape(1, n))


def _ln_proj_kernel(x_ref, sh_ref, sc_ref, w_ref, qsb_ref, ksb_ref, vsb_ref, qd_ref, kd_ref, vd_ref,
                    *, wsb, wd, sb_scale, d_scale):
    h = (_ln_rows(x_ref[...]) * (1.0 + sc_ref[...]) + sh_ref[...]).astype(BF16)

    def proj(lo, width):
        return _dot(h, w_ref[:, lo:lo + width])

    qsb_ref[...] = (proj(0, wsb) * sb_scale).astype(BF16)
    ksb_ref[...] = proj(wsb, wsb)
    vsb_ref[...] = proj(2 * wsb, wsb)
    qd_ref[...] = (proj(3 * wsb, wd) * d_scale).astype(BF16)
    kd_ref[...] = proj(3 * wsb + wd, wd)
    vd_ref[...] = proj(3 * wsb + 2 * wd, wd)


def _ln_proj(x, mod, mod_spec, w_in_bf, *, tm, wsb, wd, sb_scale, d_scale):
    t, d = x.shape
    kern = functools.partial(_ln_proj_kernel, wsb=wsb, wd=wd, sb_scale=sb_scale, d_scale=d_scale)
    row = lambda i: (i, 0)
    return pl.pallas_call(
        kern,
        grid=(t // tm,),
        in_specs=[pl.BlockSpec((tm, d), row), mod_spec(0), mod_spec(1),
                  pl.BlockSpec(w_in_bf.shape, lambda i: (0, 0), pipeline_mode=pl.Buffered(1))],
        out_specs=[pl.BlockSpec((tm, wsb), row)] * 3 + [pl.BlockSpec((tm, wd), row)] * 3,
        out_shape=[jax.ShapeDtypeStruct((t, wsb), BF16), jax.ShapeDtypeStruct((t, wsb), F32),
                   jax.ShapeDtypeStruct((t, wsb), F32), jax.ShapeDtypeStruct((t, wd), BF16),
                   jax.ShapeDtypeStruct((t, wd), F32), jax.ShapeDtypeStruct((t, wd), F32)],
        compiler_params=_params(("arbitrary",)),
        name="ln_proj",
    )(x, mod, mod, w_in_bf)


def _attn_prompt_kernel(far_ref, lam_ref,
                        qsb_ref, ksb_ref, vsb_ref, qd_ref, kd_ref, vd_ref, bias_ref, ut_ref, g_ref,
                        osb_ref, od_ref,
                        kbs_ref, vts_ref, kbd_ref, vtd_ref,
                        c_ref, accs_ref, m1_ref, l1_ref, a1_ref, m2_ref, l2_ref, a2_ref,
                        *, bq, nq, dqk, post_scale):
    h = pl.program_id(1)
    qi = pl.program_id(2)

    @pl.when(qi == 0)
    def _():
        for jb in range(nq):
            rs = slice(jb * bq, (jb + 1) * bq)
            kbs_ref[jb] = ksb_ref[rs, :].astype(BF16)
            kbd_ref[jb] = kd_ref[rs, :].astype(BF16)
            vts_ref[jb] = vsb_ref[rs, :].T.astype(BF16)
            vtd_ref[jb] = vd_ref[rs, :].T.astype(BF16)

    q_sb = qsb_ref[...]
    q_d = qd_ref[...]
    lane = lax.broadcasted_iota(jnp.int32, q_d.shape, 1)
    q1 = jnp.where(lane < dqk, q_d, jnp.zeros_like(q_d))
    q2 = jnp.where(lane >= dqk, q_d, jnp.zeros_like(q_d))
    ut = ut_ref[...]

    c_ref[...] = jnp.zeros_like(c_ref)
    accs_ref[...] = jnp.zeros_like(accs_ref)
    for m_ref, l_ref, a_ref in ((m1_ref, l1_ref, a1_ref), (m2_ref, l2_ref, a2_ref)):
        m_ref[...] = jnp.full_like(m_ref, NEG_BIG)
        l_ref[...] = jnp.zeros_like(l_ref)
        a_ref[...] = jnp.zeros_like(a_ref)

    def block(jb, bias, diag):
        if diag:
            s_i = lax.broadcasted_iota(jnp.int32, (bq, bq), 0)
            t_i = lax.broadcasted_iota(jnp.int32, (bq, bq), 1)
            strict, causal = s_i < t_i, s_i <= t_i
        z = _dot_t(kbs_ref[jb], q_sb)
        sp = _softplus(z)
        log_1m = -sp
        if diag:
            log_1m = jnp.where(strict, log_1m, 0.0)
        hi, lo = _split_bf16(log_1m)
        later = _dot(ut, hi) + _dot(ut, lo)
        a = jnp.exp((z - sp) + later + c_ref[...])
        if diag:
            a = jnp.where(strict, a, 0.0)
        accs_ref[...] += _dot(vts_ref[jb], a.astype(BF16))
        c_ref[...] += jnp.sum(log_1m, axis=0, keepdims=True)
        kd = kbd_ref[jb]
        vt = vtd_ref[jb]
        for qm, m_ref, l_ref, a_ref in ((q1, m1_ref, l1_ref, a1_ref), (q2, m2_ref, l2_ref, a2_ref)):
            sc = _dot_t(kd, qm) + bias
            if diag:
                sc = jnp.where(causal, sc, NEG_BIG)
            m_old = m_ref[...]
            m_new = jnp.maximum(m_old, jnp.max(sc, axis=0, keepdims=True))
            alpha = jnp.exp(m_old - m_new)
            p = jnp.exp(sc - m_new)
            l_ref[...] = alpha * l_ref[...] + jnp.sum(p, axis=0, keepdims=True)
            a_ref[...] = alpha * a_ref[...] + _dot(vt, p.astype(BF16))
            m_ref[...] = m_new

    block(qi, bias_ref[0], True)

    @pl.when(qi >= 1)
    def _():
        block(qi - 1, bias_ref[1], False)

    far = far_ref[h]
    n_far = jnp.maximum(qi - 1, 0)

    n_quad = lax.shift_right_logical(n_far, 2)
    n_rest = n_far - 4 * n_quad

    def far_quad(t, carry):
        for u in range(4):
            block(qi - 2 - 4 * t - u, far, False)
        return carry

    lax.fori_loop(0, n_quad, far_quad, 0)

    def far_single(t, carry):
        block(n_rest - 1 - t, far, False)
        return carry

    lax.fori_loop(0, n_rest, far_single, 0)

    osb_ref[...] = accs_ref[...].T.astype(osb_ref.dtype)
    o_d = (a1_ref[...] / l1_ref[...] - lam_ref[0] * (a2_ref[...] / l2_ref[...])).T
    od_ref[...] = _sub_rms(o_d, g_ref[...], post_scale).astype(od_ref.dtype)


def _attn_prompt(q_sb, k_sb, v_sb, q_d, k_d, v_d, bias_tiles, far_bias, lam, ut_mat, subln_g,
                 *, batch, seq, heads, bq, post_scale):
    t, w = q_sb.shape
    dh = w // heads
    nq = seq // bq
    kern = functools.partial(_attn_prompt_kernel, bq=bq, nq=nq, dqk=dh // 2, post_scale=post_scale)
    qmap = lambda b, h, i, *_: (b * nq + i, h)
    kvmap = lambda b, h, i, *_: (b, h)
    row_stat = pltpu.VMEM((1, bq), F32)
    grid_spec = pltpu.PrefetchScalarGridSpec(
        num_scalar_prefetch=2,
        grid=(batch, heads, nq),
        in_specs=[pl.BlockSpec((bq, dh), qmap), pl.BlockSpec((seq, dh), kvmap), pl.BlockSpec((seq, dh), kvmap),
                  pl.BlockSpec((bq, dh), qmap), pl.BlockSpec((seq, dh), kvmap), pl.BlockSpec((seq, dh), kvmap),
                  pl.BlockSpec((None, 2, bq, bq), lambda b, h, i, *_: (h, 0, 0, 0)),
                  pl.BlockSpec((bq, bq), lambda b, h, i, *_: (0, 0)),
                  pl.BlockSpec((1, dh), lambda b, h, i, *_: (0, 0))],
        out_specs=[pl.BlockSpec((bq, dh), qmap), pl.BlockSpec((bq, dh), qmap)],
        scratch_shapes=[pltpu.VMEM((nq, bq, dh), BF16), pltpu.VMEM((nq, dh, bq), BF16),
                        pltpu.VMEM((nq, bq, dh), BF16), pltpu.VMEM((nq, dh, bq), BF16),
                        row_stat, pltpu.VMEM((dh, bq), F32),
                        row_stat, row_stat, pltpu.VMEM((dh, bq), F32),
                        row_stat, row_stat, pltpu.VMEM((dh, bq), F32)],
    )
    return pl.pallas_call(
        kern,
        grid_spec=grid_spec,
        out_shape=[jax.ShapeDtypeStruct((t, w), BF16), jax.ShapeDtypeStruct((t, w), BF16)],
        compiler_params=_params(("arbitrary", "arbitrary", "arbitrary")),
        name="attn_prompt",
    )(far_bias, lam, q_sb, k_sb, v_sb, q_d, k_d, v_d, bias_tiles, ut_mat, subln_g)


def _attn_sample_kernel(pt_ref, lam_ref,
                        qs_ref, qd_ref, kns_ref, vns_ref, knd_ref, vnd_ref,
                        bnew_ref, blast_ref, bfar_ref, u_ref, ones_ref, g_ref, *rest,
                        n_pg, heads, dec_seq, page, post_scale):
    pages = rest[:4 * n_pg]
    osb_ref, od_ref = rest[4 * n_pg:4 * n_pg + 2]
    c_ref, accs_ref, m_ref, l_ref, accd_ref, kss_ref, vss_ref, kds_ref, vds_ref = rest[4 * n_pg + 2:]
    ks_pages, vs_pages = pages[0:n_pg], pages[n_pg:2 * n_pg]
    kd_pages, vd_pages = pages[2 * n_pg:3 * n_pg], pages[3 * n_pg:4 * n_pg]
    j = pl.program_id(1)
    nj = pl.num_programs(1)
    rq = BF16_ROWS
    dh = g_ref.shape[1]
    u_mat = u_ref[...]
    ones = ones_ref[...]

    def pv_all(p_list, values):
        full = _dot(jnp.concatenate([p.astype(BF16) for p in p_list], axis=1), values)
        return jnp.concatenate([full[hh * rq:(hh + 1) * rq, hh * dh:(hh + 1) * dh] for hh in range(heads)], axis=0)

    def sb_keys(z, carry, strict):
        sp = _softplus(z)
        log_1m = -sp
        if strict is not None:
            log_1m = jnp.where(strict, log_1m, 0.0)
        hi, lo = _split_bf16(log_1m)
        later = _dot(hi, u_mat) + _dot(lo, u_mat)
        a = jnp.exp((z - sp) + later + carry)
        if strict is not None:
            a = jnp.where(strict, a, 0.0)
        return a, carry + (_dot(hi, ones) + _dot(lo, ones))

    def softmax_update(s_list, values, m_old, l_old, acc_old):
        mx = functools.reduce(jnp.maximum, [jnp.max(s, axis=1, keepdims=True) for s in s_list])
        m_new = jnp.maximum(m_old, mx)
        alpha = jnp.exp(m_old - m_new)
        p_list = [jnp.exp(s - m_new) for s in s_list]
        l_new = alpha * l_old + functools.reduce(
            lambda a, b: a + b, [jnp.sum(p, axis=1, keepdims=True) for p in p_list])
        return m_new, l_new, alpha * acc_old + pv_all(p_list, values)

    def stage(page_refs, dst_ref):
        for p, ref in enumerate(page_refs):
            dst_ref[p * page:(p + 1) * page, :] = jnp.concatenate(
                [ref[pl.ds(hh, page, stride=heads), :] for hh in range(heads)], axis=1).astype(BF16)

    def page_scores(keys_ref, q_ref):
        zt = _dot_t(keys_ref[...], q_ref[...])
        return [zt[p * page:(p + 1) * page, :].T for p in range(n_pg)]

    @pl.when(j == 0)
    def _():
        shape = (heads * rq, kns_ref.shape[0])
        r_i = lax.broadcasted_iota(jnp.int32, shape, 0)
        s_i = lax.broadcasted_iota(jnp.int32, shape, 1)
        a, carry = sb_keys(_dot_t(qs_ref[...], kns_ref[...]), jnp.zeros(shape, F32), s_i < r_i % rq)
        accs_ref[...] = pv_all([a], vns_ref[...])
        c_ref[...] = carry
        sc = _dot_t(qd_ref[...], knd_ref[...]) + bnew_ref[...]
        sc = jnp.where(s_i <= r_i % dec_seq, sc, NEG_BIG)
        m_new, l_new, acc = softmax_update(
            [sc], vnd_ref[...], jnp.full((shape[0], 1), NEG_BIG, F32),
            jnp.zeros((shape[0], 1), F32), jnp.zeros(accd_ref.shape, F32))
        m_ref[...] = m_new
        l_ref[...] = l_new
        accd_ref[...] = acc

    stage(ks_pages, kss_ref)
    stage(vs_pages, vss_ref)
    stage(kd_pages, kds_ref)
    stage(vd_pages, vds_ref)

    carry = c_ref[...]
    a_list = []
    for z in page_scores(kss_ref, qs_ref):
        a, carry = sb_keys(z, carry, None)
        a_list.append(a)
    accs_ref[...] += pv_all(a_list, vss_ref[...])
    c_ref[...] = carry

    s_list = page_scores(kds_ref, qd_ref)
    s_list = [s + (jnp.where(j == 0, blast_ref[...], bfar_ref[...]) if p == 0 else bfar_ref[...])
              for p, s in enumerate(s_list)]
    m_new, l_new, acc = softmax_update(s_list, vds_ref[...], m_ref[...], l_ref[...], accd_ref[...])
    m_ref[...] = m_new
    l_ref[...] = l_new
    accd_ref[...] = acc

    @pl.when(j == nj - 1)
    def _():
        acc_s = accs_ref[...]
        dn = accd_ref[...] / l_ref[...]
        g = g_ref[...]
        for hh in range(heads):
            cs = slice(hh * dh, (hh + 1) * dh)
            osb_ref[:, cs] = acc_s[hh * rq:hh * rq + dec_seq].astype(osb_ref.dtype)
            o_d = dn[hh * rq:hh * rq + dec_seq] - lam_ref[0] * dn[hh * rq + dec_seq:hh * rq + 2 * dec_seq]
            od_ref[:, cs] = _sub_rms(o_d, g, post_scale).astype(od_ref.dtype)


def _attn_sample(page_table, lam, q_sb16, q_d16, new_kv, caches, bias_new, bias_last, bias_far, u_mat, ones_mat,
                 subln_g, *, layer, depth, heads, dec_seq, page, n_pg, post_scale):
    nb, n_pages = page_table.shape
    rows, w = q_sb16.shape[1:]
    dh = w // heads
    nj = n_pages // n_pg
    kern = functools.partial(_attn_sample_kernel, n_pg=n_pg, heads=heads, dec_seq=dec_seq, page=page,
                             post_scale=post_scale)
    bmap = lambda b, j, *_: (b, 0, 0)
    const = lambda b, j, *_: (0, 0)

    def page_spec(p):
        def imap(b, j, pt, lam_):
            return (pt[b * n_pages + (n_pages - 1 - (j * n_pg + p))] * depth + layer, 0, 0)
        return pl.BlockSpec((None, page * heads, dh), imap)

    in_specs = [pl.BlockSpec((None, rows, w), bmap)] * 2
    in_specs += [pl.BlockSpec((None, page, w), bmap)] * 4
    in_specs += [pl.BlockSpec((rows, page), const)] * 3 + [pl.BlockSpec((page, page), const)] * 2
    in_specs += [pl.BlockSpec((1, dh), const)]
    for _ in range(4):
        in_specs += [page_spec(p) for p in range(n_pg)]
    omap = lambda b, j, *_: (b, 0)
    grid_spec = pltpu.PrefetchScalarGridSpec(
        num_scalar_prefetch=2,
        grid=(nb, nj),
        in_specs=in_specs,
        out_specs=[pl.BlockSpec((dec_seq, w), omap), pl.BlockSpec((dec_seq, w), omap)],
        scratch_shapes=[pltpu.VMEM((rows, page), F32), pltpu.VMEM((rows, dh), F32),
                        pltpu.VMEM((rows, 1), F32), pltpu.VMEM((rows, 1), F32), pltpu.VMEM((rows, dh), F32)]
        + [pltpu.VMEM((n_pg * page, w), BF16)] * 4,
    )
    page_args = []
    for c in caches:
        page_args += [c] * n_pg
    return pl.pallas_call(
        kern,
        grid_spec=grid_spec,
        out_shape=[jax.ShapeDtypeStruct((nb * dec_seq, w), F32)] * 2,
        compiler_params=_params(("arbitrary", "arbitrary")),
        name="attn_sample",
    )(page_table.reshape(-1), lam, q_sb16, q_d16, *new_kv, bias_new, bias_last, bias_far, u_mat, ones_mat,
      subln_g, *page_args)


def _out_router_kernel(osb_ref, od_ref, x_ref, g1_ref, sh2_ref, sc2_ref, wo_ref, lng_ref, lnb_ref,
                       wrh_ref, wrl_ref, br_ref, x1_ref, h2_ref, idx_ref, gate_ref, *, wsb, alpha):
    o = (_dot(osb_ref[...].astype(BF16), wo_ref[:wsb, :]) + _dot(od_ref[...].astype(BF16), wo_ref[wsb:, :]))
    x1 = _ln_rows(alpha * x_ref[...] + g1_ref[...] * o) * lng_ref[...] + lnb_ref[...]
    x1_ref[...] = x1
    h2 = _ln_rows(x1) * (1.0 + sc2_ref[...]) + sh2_ref[...]
    _pack_rows(h2, h2_ref, h2.shape[0])
    hi, lo = _split_bf16(h2)
    logits = (_dot(hi, wrh_ref[...]) + _dot(lo, wrh_ref[...]) + _dot(hi, wrl_ref[...])) + br_ref[...]
    lane = lax.broadcasted_iota(jnp.int32, logits.shape, 1)
    lane_f = lane.astype(F32)
    vals, idxs = [], []
    for _ in range(TOP_K):
        mx = jnp.max(logits, axis=1, keepdims=True)
        ix = jnp.min(jnp.where(logits == mx, lane_f, float(logits.shape[1])), axis=1, keepdims=True)
        vals.append(mx)
        idxs.append(ix.astype(jnp.int32))
        logits = jnp.where(lane_f == ix, -jnp.inf, logits)
    exps = [jnp.exp(v - vals[0]) for v in vals]
    denom = functools.reduce(lambda a, b: a + b, exps)
    idx_out = jnp.zeros(lane.shape, jnp.int32)
    gate_out = jnp.zeros(lane.shape, F32)
    for k in range(TOP_K):
        idx_out = jnp.where(lane == k, idxs[k], idx_out)
        gate_out = jnp.where(lane == k, exps[k] / denom, gate_out)
    idx_ref[...] = idx_out
    gate_ref[...] = gate_out


def _out_router(o_sb, o_d, x, mod, mod_spec, w_out_bf, ln_g, ln_b, wr_hi, wr_lo, br_pad, *, tm, alpha):
    t, d = x.shape
    wsb = o_sb.shape[1]
    ne = wr_hi.shape[1]
    nsub = d // (2 * LANES)
    row = lambda i: (i, 0)
    const = lambda i: (0, 0)
    kern = functools.partial(_out_router_kernel, wsb=wsb, alpha=alpha)
    return pl.pallas_call(
        kern,
        grid=(t // tm,),
        in_specs=[pl.BlockSpec((tm, wsb), row), pl.BlockSpec((tm, o_d.shape[1]), row), pl.BlockSpec((tm, d), row),
                  mod_spec(2), mod_spec(3), mod_spec(4),
                  pl.BlockSpec(w_out_bf.shape, const, pipeline_mode=pl.Buffered(1)),
                  pl.BlockSpec((1, d), const), pl.BlockSpec((1, d), const),
                  pl.BlockSpec((d, ne), const), pl.BlockSpec((d, ne), const), pl.BlockSpec((1, ne), const)],
        out_specs=[pl.BlockSpec((tm, d), row), pl.BlockSpec((tm * nsub, LANES), row),
                   pl.BlockSpec((tm, ne), row), pl.BlockSpec((tm, ne), row)],
        out_shape=[jax.ShapeDtypeStruct((t, d), F32), jax.ShapeDtypeStruct((t * nsub, LANES), U32),
                   jax.ShapeDtypeStruct((t, ne), jnp.int32), jax.ShapeDtypeStruct((t, ne), F32)],
        compiler_params=_params(("arbitrary",)),
        name="out_router",
    )(o_sb, o_d, x, mod, mod, mod, w_out_bf, ln_g, ln_b, wr_hi, wr_lo, br_pad)


def _row_copy(src_ref, src_row, dst_ref, dst_row, nsub, sem):
    def rows(r):
        return pl.ds(r * nsub if isinstance(r, int) else pl.multiple_of(r * nsub, nsub), nsub)

    return pltpu.make_async_copy(src_ref.at[rows(src_row)], dst_ref.at[rows(dst_row)], sem)


def _dispatch_kernel(rowp_ref, x_ref, init_ref, dst_ref, sem, *, chunk, top_k, nsub):
    del init_ref
    n_tok = chunk // top_k

    def start(t, carry):
        for k in range(top_k):
            _row_copy(x_ref, t, dst_ref, rowp_ref[t * top_k + k], nsub, sem).start()
        return carry

    lax.fori_loop(0, n_tok, start, 0)
    for _ in range(top_k):
        pltpu.make_async_copy(x_ref, dst_ref.at[pl.ds(0, n_tok * nsub)], sem).wait()


def _dispatch(row_p, h2_packed, n_rows, *, chunk, top_k, nsub):
    n_pairs = row_p.shape[0]
    init = jnp.zeros((n_rows * nsub, LANES), U32)
    kern = functools.partial(_dispatch_kernel, chunk=chunk, top_k=top_k, nsub=nsub)
    return pl.pallas_call(
        kern,
        grid=(n_pairs // chunk,),
        in_specs=[pl.BlockSpec((chunk,), lambda i: (i,), memory_space=pltpu.SMEM),
                  pl.BlockSpec((chunk // top_k * nsub, LANES), lambda i: (i, 0)),
                  pl.BlockSpec(memory_space=pl.ANY)],
        out_specs=pl.BlockSpec(memory_space=pl.ANY),
        out_shape=jax.ShapeDtypeStruct(init.shape, U32),
        scratch_shapes=[pltpu.SemaphoreType.DMA],
        input_output_aliases={2: 0},
        compiler_params=_params(("arbitrary",)),
        name="dispatch",
    )(row_p, h2_packed, init)


def _moe_up_kernel(te_ref, tf_ref, rb_ref, nu_ref, x_ref, w_ref, b_ref, perm_ref, o_ref, wb_ref, *, tm, nsub):
    i = pl.program_id(1)

    @pl.when(tf_ref[i] == 1)
    def _():
        wb_ref[...] = w_ref[...].astype(BF16)

    @pl.when(i < nu_ref[0])
    def _():
        hi, lo = _unpack_rows(x_ref, 0, tm, nsub)
        x = jnp.concatenate(hi + lo, axis=1).astype(BF16)
        u = _dot(x, wb_ref[...]) + b_ref[...]
        even = (lax.broadcasted_iota(jnp.int32, (tm, LANES), 1) & 1) == 0
        perm = perm_ref[...]
        for c in range(u.shape[1] // (2 * LANES)):
            a = u[:, 2 * c * LANES:(2 * c + 1) * LANES]
            b = u[:, (2 * c + 1) * LANES:(2 * c + 2) * LANES]
            glu = jnp.where(even, a, pltpu.roll(b, 1, 1))
            lin = jnp.where(even, pltpu.roll(a, LANES - 1, 1), b)
            glu = jnp.minimum(glu, SWIGLU_LIMIT)
            lin = jnp.clip(lin, -SWIGLU_LIMIT, SWIGLU_LIMIT)
            act = (glu * jax.nn.sigmoid(SWIGLU_ALPHA * glu) * (lin + 1.0)).astype(BF16)
            o_ref[:, c * LANES:(c + 1) * LANES] = _dot(act, perm).astype(o_ref.dtype)


def _moe_up(tile_expert, tile_first, row_block, n_used, xs, w_up, b_up, perm, *, tm, tf, nsub):
    r_pad = xs.shape[0] // nsub
    ne, d, f2 = w_up.shape
    f = f2 // 2
    n_tiles = r_pad // tm
    kern = functools.partial(_moe_up_kernel, tm=tm, nsub=nsub)
    grid_spec = pltpu.PrefetchScalarGridSpec(
        num_scalar_prefetch=4,
        grid=(f // tf, n_tiles),
        in_specs=[pl.BlockSpec((tm * nsub, LANES), lambda j, i, te, tfr, rb, nu: (rb[i], 0)),
                  pl.BlockSpec((None, d, 2 * tf), lambda j, i, te, tfr, rb, nu: (te[i], 0, j)),
                  pl.BlockSpec((None, 1, 2 * tf), lambda j, i, te, tfr, rb, nu: (te[i], 0, j)),
                  pl.BlockSpec((LANES, LANES), lambda j, i, te, tfr, rb, nu: (0, 0))],
        out_specs=pl.BlockSpec((tm, tf), lambda j, i, te, tfr, rb, nu: (rb[i], j)),
        scratch_shapes=[pltpu.VMEM((d, 2 * tf), BF16)],
    )
    return pl.pallas_call(
        kern,
        grid_spec=grid_spec,
        out_shape=jax.ShapeDtypeStruct((r_pad, f), BF16),
        compiler_params=_params(("arbitrary", "arbitrary")),
        name="moe_up",
    )(tile_expert, tile_first, row_block, n_used, xs, w_up, b_up.reshape(ne, 1, f2), perm)


def _moe_down_kernel(te_ref, tf_ref, rb_ref, nu_ref, a_ref, w_ref, b_ref, o_ref, wb_ref, *, tm):
    i = pl.program_id(0)

    @pl.when(tf_ref[i] == 1)
    def _():
        wb_ref[...] = w_ref[...].astype(BF16)

    @pl.when(i < nu_ref[0])
    def _():
        _pack_rows(_dot(a_ref[...], wb_ref[...]) + b_ref[...], o_ref, tm)


def _moe_down(tile_expert, tile_first, row_block, n_used, act, w_down, b_down, *, tm, nsub):
    r_pad, f = act.shape
    ne, _, d = w_down.shape
    n_tiles = r_pad // tm
    grid_spec = pltpu.PrefetchScalarGridSpec(
        num_scalar_prefetch=4,
        grid=(n_tiles,),
        in_specs=[pl.BlockSpec((tm, f), lambda i, te, tfr, rb, nu: (rb[i], 0)),
                  pl.BlockSpec((None, f, d), lambda i, te, tfr, rb, nu: (te[i], 0, 0)),
                  pl.BlockSpec((None, 1, d), lambda i, te, tfr, rb, nu: (te[i], 0, 0))],
        out_specs=pl.BlockSpec((tm * nsub, LANES), lambda i, te, tfr, rb, nu: (rb[i], 0)),
        scratch_shapes=[pltpu.VMEM((f, d), BF16)],
    )
    return pl.pallas_call(
        functools.partial(_moe_down_kernel, tm=tm),
        grid_spec=grid_spec,
        out_shape=jax.ShapeDtypeStruct((r_pad * nsub, LANES), U32),
        compiler_params=_params(("arbitrary",)),
        name="moe_down",
    )(tile_expert, tile_first, row_block, n_used, act, w_down, b_down.reshape(ne, 1, d))


def _route(top_i, n_exp, tm, n_tiles):
    flat_e = top_i.reshape(-1)
    onehot = (flat_e[:, None] == jnp.arange(n_exp, dtype=jnp.int32)[None, :]).astype(jnp.int32)
    csum = jnp.cumsum(onehot, axis=0)
    rank = jnp.sum((csum - onehot) * onehot, axis=1)
    counts = csum[-1]
    tiles_e = (counts + tm - 1) // tm
    tile_end = jnp.cumsum(tiles_e)
    tile_start = tile_end - tiles_e
    n_used = tile_end[-1]
    row_p = (jnp.sum(onehot * tile_start[None, :], axis=1) * tm + rank).astype(jnp.int32)
    tile_ids = jnp.arange(n_tiles, dtype=jnp.int32)
    clamped = jnp.minimum(tile_ids, n_used - 1)
    tile_expert = jnp.minimum(jnp.sum((tile_end[None, :] <= clamped[:, None]).astype(jnp.int32), axis=1), n_exp - 1)
    tile_first = (jnp.any(tile_ids[:, None] == tile_start[None, :], axis=1) & (tile_ids < n_used)).astype(jnp.int32)
    return (tile_expert.astype(jnp.int32), tile_first, clamped.astype(jnp.int32),
            n_used.reshape(1).astype(jnp.int32), row_p)


def _combine_kernel(pos_ref, x1_ref, gate_ref, g2_ref, lng_ref, lnb_ref, y_ref, o_ref, buf_ref, sem,
                    *, tm, top_k, nsub, alpha):
    def start(t, carry):
        for k in range(top_k):
            _row_copy(y_ref, pos_ref[t * top_k + k], buf_ref, k * tm + t, nsub, sem).start()
        return carry

    lax.fori_loop(0, tm, start, 0)
    pltpu.make_async_copy(y_ref.at[pl.ds(0, tm * top_k * nsub)], buf_ref, sem).wait()

    gates = gate_ref[...]
    f_hi = [jnp.zeros((tm, LANES), F32)] * nsub
    f_lo = [jnp.zeros((tm, LANES), F32)] * nsub
    for k in range(top_k):
        gk = gates[:, k:k + 1]
        hi, lo = _unpack_rows(buf_ref, k * tm * nsub, tm, nsub)
        f_hi = [acc + gk * v for acc, v in zip(f_hi, hi)]
        f_lo = [acc + gk * v for acc, v in zip(f_lo, lo)]
    f = jnp.concatenate(f_hi + f_lo, axis=1)
    y = alpha * x1_ref[...] + g2_ref[...] * f
    o_ref[...] = _ln_rows(y) * lng_ref[...] + lnb_ref[...]


def _combine_ln(pos, tile_offset, x1, gates, mod, mod_spec, ln_g, ln_b, y_rows, *, tm, top_k, nsub, alpha):
    t, d = x1.shape
    ne = gates.shape[1]
    row = lambda i: (i, 0)
    const = lambda i: (0, 0)
    kern = functools.partial(_combine_kernel, tm=tm, top_k=top_k, nsub=nsub, alpha=alpha)
    return pl.pallas_call(
        kern,
        grid=(t // tm,),
        in_specs=[pl.BlockSpec((tm * top_k,), lambda i: (i + tile_offset,), memory_space=pltpu.SMEM),
                  pl.BlockSpec((tm, d), row), pl.BlockSpec((tm, ne), lambda i: (i + tile_offset, 0)), mod_spec(5),
                  pl.BlockSpec((1, d), const), pl.BlockSpec((1, d), const),
                  pl.BlockSpec(memory_space=pl.ANY)],
        out_specs=pl.BlockSpec((tm, d), row),
        out_shape=jax.ShapeDtypeStruct((t, d), F32),
        scratch_shapes=[pltpu.VMEM((tm * top_k * nsub, LANES), U32), pltpu.SemaphoreType.DMA],
        compiler_params=_params(("arbitrary",)),
        name="combine_ln",
    )(pos, x1, gates, mod, ln_g, ln_b, y_rows)


def _t5_bucket(rel, n_buckets):
    n = jnp.maximum(rel, 0)
    max_exact = n_buckets // 2
    nf = jnp.maximum(n, 1).astype(F32)
    large = max_exact + (jnp.log(nf / max_exact) / math.log(MAX_DISTANCE / max_exact)
                         * (n_buckets - max_exact)).astype(jnp.int32)
    large = jnp.minimum(large, n_buckets - 1)
    return jnp.where(n < max_exact, n, large)


def _bias_tile(rel, rel_bias):
    n_buckets = rel_bias.shape[0]
    onehot = (_t5_bucket(rel, n_buckets)[..., None] == jnp.arange(n_buckets)).astype(F32)
    return jnp.einsum("rsk,kh->hrs", onehot, rel_bias.astype(F32), precision=lax.Precision.HIGHEST)


def _sample_queries(q, nb, dec_seq, heads, maps):
    dh = q.shape[1] // heads
    q4 = q.reshape(nb, dec_seq, heads, dh).transpose(0, 2, 1, 3)
    lane = jnp.arange(dh)
    parts = [jnp.where((lane >= lo) & (lane < hi), q4, jnp.zeros_like(q4)) for lo, hi in maps]
    q16 = jnp.concatenate(parts, axis=2)
    q16 = jnp.pad(q16, ((0, 0), (0, 0), (0, BF16_ROWS - q16.shape[2]), (0, 0)))
    eye = jnp.eye(heads, dtype=q.dtype)
    out = q16[:, :, :, None, :] * eye[None, :, None, :, None]
    return out.reshape(nb, heads * BF16_ROWS, heads * dh)


def kernel(x_prompt, x_sample, cache_k_sb, cache_v_sb, cache_k_diff, cache_v_diff, page_table, c_prompt, c_sample,
           w_ada, b_ada, w_in, w_out, rel_bias, lambda_q1, lambda_k1, lambda_q2, lambda_k2, subln_g, ln1_g, ln1_b,
           w_router, b_router, w_up, b_up, w_down, b_down, ln2_g, ln2_b):
    bp, seq, d = x_prompt.shape
    bs, dec_seq, _ = x_sample.shape
    n_pool, page, depth, h_sb, d_sb = cache_k_sb.shape
    _, _, _, h_d, d_vd = cache_k_diff.shape
    n_pages = page_table.shape[1]
    wsb, wd = h_sb * d_sb, h_d * d_vd
    n_exp = w_router.shape[-1]
    n_buckets = rel_bias.shape[0]
    d_ff = w_down.shape[2]
    alpha = (2.0 * depth) ** 0.25
    tp, ts = bp * seq, bs * dec_seq
    assert h_sb == h_d and d_sb == d_vd == LANES and page >= MAX_DISTANCE and 2 * dec_seq == BF16_ROWS
    heads = h_sb
    nsub = d // (2 * LANES)

    tm_p = min(256, seq)
    tm_s = ts
    bq = min(256, seq)
    n_pg = min(8, n_pages)
    tm_e = min(256, ts)
    tf_up = min(1024, d_ff)
    chunk = math.gcd((tp + ts) * TOP_K, 1024)
    assert seq % bq == 0 and bq >= MAX_DISTANCE and n_pages % n_pg == 0
    assert tp % tm_e == 0 and ts % tm_e == 0

    xp = x_prompt.reshape(tp, d)
    xs = x_sample.reshape(ts, d)
    c_all = jnp.concatenate([c_prompt, c_sample, jnp.zeros(((-(bp + bs)) % BF16_ROWS, d), F32)], axis=0)

    caches = [jnp.swapaxes(c, 1, 2).reshape(n_pool * depth, page * heads, d_sb) if depth > 1
              else c.reshape(n_pool, page * heads, d_sb)
              for c in (cache_k_sb, cache_v_sb, cache_k_diff, cache_v_diff)]

    ar = jnp.arange
    rel_t = ar(bq)[None, :] - ar(bq)[:, None]
    bias_prompt = jnp.stack([_bias_tile(rel_t, rel_bias), _bias_tile(rel_t + bq, rel_bias)], axis=1)
    far_bias = rel_bias[n_buckets - 1].astype(F32)
    rows16 = heads * BF16_ROWS
    t_rows = jnp.tile(ar(dec_seq), rows16 // dec_seq)
    h_rows = jnp.repeat(ar(heads), BF16_ROWS)
    pick = lambda tile: tile[h_rows, ar(rows16)]
    bias_new = pick(_bias_tile(t_rows[:, None] - ar(page)[None, :], rel_bias))
    bias_last = pick(_bias_tile(page + t_rows[:, None] - ar(page)[None, :], rel_bias))
    bias_far = jnp.broadcast_to(far_bias[h_rows][:, None], (rows16, page))
    later = lambda n: (ar(n)[:, None] > ar(n)[None, :]).astype(BF16)
    ut_blk = later(bq).T
    u_page = later(page)
    ones_page = jnp.ones((page, page), BF16)
    lane = ar(LANES)
    perm = (lane[None, :] == (lane[:, None] // 2 + (LANES // 2) * (lane[:, None] % 2))).astype(BF16)

    w_in_bf = w_in.astype(BF16)
    w_out_bf = w_out.astype(BF16)

    prompt_rows, sample_rows = [], []
    for l in range(depth):
        ada = _ada(c_all, w_ada[l], b_ada[l])
        mod_p = ada[:bp].reshape(bp, 1, 6 * d)
        mod_s = jnp.repeat(ada[bp:bp + bs], dec_seq, axis=0).reshape(1, ts, 6 * d)
        tiles_per_b = seq // tm_p
        spec_p = lambda k: pl.BlockSpec((None, 1, d), lambda i, k=k: (i // tiles_per_b, 0, k))
        spec_s = lambda k: pl.BlockSpec((None, tm_s, d), lambda i, k=k: (0, i, k))

        lam_init = 0.8 - 0.6 * math.exp(-0.3 * l)
        lam = (jnp.exp(jnp.sum(lambda_q1[l].astype(F32) * lambda_k1[l].astype(F32)))
               - jnp.exp(jnp.sum(lambda_q2[l].astype(F32) * lambda_k2[l].astype(F32))) + lam_init).reshape(1)
        g_sub = subln_g[l].reshape(1, d_vd).astype(F32)
        post = 1.0 - lam_init

        proj_kw = dict(wsb=wsb, wd=wd, sb_scale=d_sb ** -0.5, d_scale=(d_vd // 2) ** -0.5)
        qsb_p, ksb_p, vsb_p, qd_p, kd_p, vd_p = _ln_proj(xp, mod_p, spec_p, w_in_bf[l], tm=tm_p, **proj_kw)
        qsb_s, ksb_s, vsb_s, qd_s, kd_s, vd_s = _ln_proj(xs, mod_s, spec_s, w_in_bf[l], tm=tm_s, **proj_kw)
        prompt_rows.append((ksb_p, vsb_p, kd_p, vd_p))
        sample_rows.append((ksb_s, vsb_s, kd_s, vd_s))

        osb_p, od_p = _attn_prompt(qsb_p, ksb_p, vsb_p, qd_p, kd_p, vd_p, bias_prompt, far_bias, lam, ut_blk, g_sub,
                                   batch=bp, seq=seq, heads=heads, bq=bq, post_scale=post)

        pad_new = lambda a: jnp.pad(a.reshape(bs, dec_seq, -1).astype(BF16), ((0, 0), (0, page - dec_seq), (0, 0)))
        q_sb16 = _sample_queries(qsb_s, bs, dec_seq, heads, [(0, d_sb)])
        q_d16 = _sample_queries(qd_s, bs, dec_seq, heads, [(0, d_vd // 2), (d_vd // 2, d_vd)])
        osb_s, od_s = _attn_sample(page_table, lam, q_sb16, q_d16,
                                   [pad_new(a) for a in (ksb_s, vsb_s, kd_s, vd_s)], caches,
                                   bias_new, bias_last, bias_far, u_page, ones_page, g_sub,
                                   layer=l, depth=depth, heads=heads, dec_seq=dec_seq, page=page, n_pg=n_pg,
                                   post_scale=post)

        ne_pad = LANES
        wr = jnp.pad(w_router[l].astype(F32), ((0, 0), (0, ne_pad - n_exp)))
        wr_hi = wr.astype(BF16)
        wr_lo = (wr - wr_hi.astype(F32)).astype(BF16)
        br_pad = jnp.pad(b_router[l].astype(F32), (0, ne_pad - n_exp), constant_values=NEG_BIG).reshape(1, ne_pad)
        ln1 = (ln1_g[l].reshape(1, d), ln1_b[l].reshape(1, d))
        x1_p, h2_p, idx_p, gate_p = _out_router(osb_p, od_p, xp, mod_p, spec_p, w_out_bf[l], *ln1,
                                                wr_hi, wr_lo, br_pad, tm=tm_p, alpha=alpha)
        x1_s, h2_s, idx_s, gate_s = _out_router(osb_s, od_s, xs, mod_s, spec_s, w_out_bf[l], *ln1,
                                                wr_hi, wr_lo, br_pad, tm=tm_s, alpha=alpha)

        h2 = jnp.concatenate([h2_p, h2_s], axis=0)
        top_i = jnp.concatenate([idx_p[:, :TOP_K], idx_s[:, :TOP_K]], axis=0)
        gates = jnp.concatenate([gate_p, gate_s], axis=0)
        n_tiles = (tp + ts) * TOP_K // tm_e + n_exp
        tile_expert, tile_first, row_block, n_used, row_p = _route(top_i, n_exp, tm_e, n_tiles)
        x_rows = _dispatch(row_p, h2, n_tiles * tm_e, chunk=chunk, top_k=TOP_K, nsub=nsub)
        act = _moe_up(tile_expert, tile_first, row_block, n_used, x_rows, w_up[l], b_up[l], perm,
                      tm=tm_e, tf=tf_up, nsub=nsub)
        y_rows = _moe_down(tile_expert, tile_first, row_block, n_used, act, w_down[l], b_down[l], tm=tm_e, nsub=nsub)

        ln2 = (ln2_g[l].reshape(1, d), ln2_b[l].reshape(1, d))
        comb_kw = dict(tm=tm_e, top_k=TOP_K, nsub=nsub, alpha=alpha)
        tiles_per_b_e = seq // tm_e
        spec_pe = lambda k: pl.BlockSpec((None, 1, d), lambda i, k=k: (i // tiles_per_b_e, 0, k))
        spec_se = lambda k: pl.BlockSpec((None, tm_e, d), lambda i, k=k: (0, i, k))
        xp = _combine_ln(row_p, 0, x1_p, gates, mod_p, spec_pe, *ln2, y_rows, **comb_kw)
        xs = _combine_ln(row_p, tp // tm_e, x1_s, gates, mod_s, spec_se, *ln2, y_rows, **comb_kw)

    def stack_rows(rows, i, nb, t, h, dh):
        return jnp.stack([r[i].reshape(nb, t, h, dh) for r in rows], axis=2)

    return (xp.reshape(bp, seq, d), xs.reshape(bs, dec_seq, d),
            stack_rows(prompt_rows, 0, bp, seq, h_sb, d_sb), stack_rows(prompt_rows, 1, bp, seq, h_sb, d_sb),
            stack_rows(prompt_rows, 2, bp, seq, h_d, d_vd), stack_rows(prompt_rows, 3, bp, seq, h_d, d_vd),
            stack_rows(sample_rows, 0, bs, dec_seq, h_sb, d_sb), stack_rows(sample_rows, 1, bs, dec_seq, h_sb, d_sb),
            stack_rows(sample_rows, 2, bs, dec_seq, h_d, d_vd), stack_rows(sample_rows, 3, bs, dec_seq, h_d, d_vd))
```

```python
import functools
import math

import jax
import jax.numpy as jnp
from jax import lax
from jax.experimental import pallas as pl
from jax.experimental.pallas import tpu as pltpu

LN_EPS = 1e-5
TOP_K = 4
MAX_DISTANCE = 128
SWIGLU_LIMIT = 7.0
SWIGLU_ALPHA = 1.702
NEG_BIG = -1e30
LANES = 128
SUBLANES = 8
BF16_ROWS = 16
VMEM_LIMIT = 56 * 1024 * 1024

BF16 = jnp.bfloat16
F32 = jnp.float32
U32 = jnp.uint32


def _params(sem, vmem=VMEM_LIMIT):
    return pltpu.CompilerParams(dimension_semantics=sem, vmem_limit_bytes=vmem)


def _ln_rows(x):
    mu = jnp.mean(x, axis=-1, keepdims=True)
    xc = x - mu
    var = jnp.mean(xc * xc, axis=-1, keepdims=True)
    return xc * lax.rsqrt(var + LN_EPS)


def _dot(a, b):
    return jnp.dot(a, b, preferred_element_type=F32)


def _dot_t(a, b):
    return lax.dot_general(a, b, (((1,), (1,)), ((), ())), preferred_element_type=F32)


def _softplus(z):
    return jnp.maximum(z, 0.0) + jnp.log(1.0 + jnp.exp(-jnp.abs(z)))


def _split_bf16(x):
    hi = x.astype(BF16)
    lo = (x - hi.astype(F32)).astype(BF16)
    return hi, lo


def _sub_rms(o, g, post_scale):
    ms = jnp.mean(o * o, axis=-1, keepdims=True)
    return o * lax.rsqrt(ms + LN_EPS) * g * post_scale


def _pack_word(a, b, dst_ref, s, n_rows, nsub):
    a = a.astype(BF16).astype(F32)
    b = b.astype(BF16).astype(F32)
    w = lax.bitcast_convert_type(a, U32) | (lax.bitcast_convert_type(b, U32) >> 16)
    dst_ref[pl.ds(s, n_rows, stride=nsub), :] = w


def _pack_rows(y, dst_ref, n_rows):
    half = y.shape[1] // 2
    nsub = half // LANES
    for s in range(nsub):
        _pack_word(y[:, s * LANES:(s + 1) * LANES], y[:, half + s * LANES:half + (s + 1) * LANES],
                   dst_ref, s, n_rows, nsub)


def _unpack_rows(src_ref, start, n_rows, nsub):
    hi, lo = [], []
    for s in range(nsub):
        w = src_ref[pl.ds(start + s, n_rows, stride=nsub), :]
        hi.append(lax.bitcast_convert_type(w & jnp.uint32(0xFFFF0000), F32))
        lo.append(lax.bitcast_convert_type(w << 16, F32))
    return hi, lo


def _ada_kernel(c_ref, w_ref, b_ref, o_ref):
    c = c_ref[...]
    s = (c * jax.nn.sigmoid(c)).astype(BF16)
    o_ref[...] = _dot(s, w_ref[...].astype(BF16)) + b_ref[...]


def _ada(c_all, w_ada, b_ada):
    rows, d = c_all.shape
    n = w_ada.shape[1]
    tn = min(n, 1024)
    return pl.pallas_call(
        _ada_kernel,
        grid=(n // tn,),
        in_specs=[pl.BlockSpec((rows, d), lambda j: (0, 0)),
                  pl.BlockSpec((d, tn), lambda j: (0, j)),
                  pl.BlockSpec((1, tn), lambda j: (0, j))],
        out_specs=pl.BlockSpec((rows, tn), lambda j: (0, j)),
        out_shape=jax.ShapeDtypeStruct((rows, n), F32),
        compiler_params=_params(("arbitrary",)),
        name="ada",
    )(c_all, w_ada, b_ada.reshape(1, n))


def _ln_proj_kernel(x_ref, sh_ref, sc_ref, w_ref, qsb_ref, ksb_ref, vsb_ref, qd_ref, kd_ref, vd_ref,
                    *, wsb, wd, sb_scale, d_scale):
    h = (_ln_rows(x_ref[...]) * (1.0 + sc_ref[...]) + sh_ref[...]).astype(BF16)

    def proj(lo, width):
        return _dot(h, w_ref[:, lo:lo + width])

    qsb_ref[...] = (proj(0, wsb) * sb_scale).astype(BF16)
    ksb_ref[...] = proj(wsb, wsb)
    vsb_ref[...] = proj(2 * wsb, wsb)
    qd_ref[...] = (proj(3 * wsb, wd) * d_scale).astype(BF16)
    kd_ref[...] = proj(3 * wsb + wd, wd)
    vd_ref[...] = proj(3 * wsb + 2 * wd, wd)


def _ln_proj(x, mod, mod_spec, w_in_bf, *, tm, wsb, wd, sb_scale, d_scale):
    t, d = x.shape
    kern = functools.partial(_ln_proj_kernel, wsb=wsb, wd=wd, sb_scale=sb_scale, d_scale=d_scale)
    row = lambda i: (i, 0)
    return pl.pallas_call(
        kern,
        grid=(t // tm,),
        in_specs=[pl.BlockSpec((tm, d), row), mod_spec(0), mod_spec(1),
                  pl.BlockSpec(w_in_bf.shape, lambda i: (0, 0), pipeline_mode=pl.Buffered(1))],
        out_specs=[pl.BlockSpec((tm, wsb), row)] * 3 + [pl.BlockSpec((tm, wd), row)] * 3,
        out_shape=[jax.ShapeDtypeStruct((t, wsb), BF16), jax.ShapeDtypeStruct((t, wsb), F32),
                   jax.ShapeDtypeStruct((t, wsb), F32), jax.ShapeDtypeStruct((t, wd), BF16),
                   jax.ShapeDtypeStruct((t, wd), F32), jax.ShapeDtypeStruct((t, wd), F32)],
        compiler_params=_params(("arbitrary",)),
        name="ln_proj",
    )(x, mod, mod, w_in_bf)


def _attn_prompt_kernel(far_ref, lam_ref,
                        qsb_ref, ksb_ref, vsb_ref, qd_ref, kd_ref, vd_ref, bias_ref, ut_ref, g_ref,
                        osb_ref, od_ref,
                        kbs_ref, vts_ref, kbd_ref, vtd_ref,
                        c_ref, accs_ref, m1_ref, l1_ref, a1_ref, m2_ref, l2_ref, a2_ref,
                        *, bq, nq, dqk, post_scale, far_unroll):
    h = pl.program_id(1)
    qi = pl.program_id(2)

    @pl.when(qi == 0)
    def _():
        for jb in range(nq):
            rs = slice(jb * bq, (jb + 1) * bq)
            kbs_ref[jb] = ksb_ref[rs, :].astype(BF16)
            kbd_ref[jb] = kd_ref[rs, :].astype(BF16)
            vts_ref[jb] = vsb_ref[rs, :].T.astype(BF16)
            vtd_ref[jb] = vd_ref[rs, :].T.astype(BF16)

    q_sb = qsb_ref[...]
    q_d = qd_ref[...]
    lane = lax.broadcasted_iota(jnp.int32, q_d.shape, 1)
    q1 = jnp.where(lane < dqk, q_d, jnp.zeros_like(q_d))
    q2 = jnp.where(lane >= dqk, q_d, jnp.zeros_like(q_d))
    ut = ut_ref[...]

    c_ref[...] = jnp.zeros_like(c_ref)
    accs_ref[...] = jnp.zeros_like(accs_ref)
    for m_ref, l_ref, a_ref in ((m1_ref, l1_ref, a1_ref), (m2_ref, l2_ref, a2_ref)):
        m_ref[...] = jnp.full_like(m_ref, NEG_BIG)
        l_ref[...] = jnp.zeros_like(l_ref)
        a_ref[...] = jnp.zeros_like(a_ref)

    def block(jb, bias, diag):
        if diag:
            s_i = lax.broadcasted_iota(jnp.int32, (bq, bq), 0)
            t_i = lax.broadcasted_iota(jnp.int32, (bq, bq), 1)
            strict, causal = s_i < t_i, s_i <= t_i
        z = _dot_t(kbs_ref[jb], q_sb)
        sp = _softplus(z)
        log_1m = -sp
        if diag:
            log_1m = jnp.where(strict, log_1m, 0.0)
        hi, lo = _split_bf16(log_1m)
        later = _dot(ut, hi) + _dot(ut, lo)
        a = jnp.exp((z - sp) + later + c_ref[...])
        if diag:
            a = jnp.where(strict, a, 0.0)
        accs_ref[...] += _dot(vts_ref[jb], a.astype(BF16))
        c_ref[...] += jnp.sum(log_1m, axis=0, keepdims=True)
        kd = kbd_ref[jb]
        vt = vtd_ref[jb]
        for qm, m_ref, l_ref, a_ref in ((q1, m1_ref, l1_ref, a1_ref), (q2, m2_ref, l2_ref, a2_ref)):
            sc = _dot_t(kd, qm) + bias
            if diag:
                sc = jnp.where(causal, sc, NEG_BIG)
            m_old = m_ref[...]
            m_new = jnp.maximum(m_old, jnp.max(sc, axis=0, keepdims=True))
            alpha = jnp.exp(m_old - m_new)
            p = jnp.exp(sc - m_new)
            l_ref[...] = alpha * l_ref[...] + jnp.sum(p, axis=0, keepdims=True)
            a_ref[...] = alpha * a_ref[...] + _dot(vt, p.astype(BF16))
            m_ref[...] = m_new

    block(qi, bias_ref[0], True)

    @pl.when(qi >= 1)
    def _():
        block(qi - 1, bias_ref[1], False)

    far = far_ref[h]
    n_far = jnp.maximum(qi - 1, 0)

    if far_unroll > 1:
        n_main = lax.shift_right_logical(n_far, far_unroll.bit_length() - 1)
        n_rest = n_far - far_unroll * n_main

        def far_main(t, carry):
            for u in range(far_unroll):
                block(qi - 2 - far_unroll * t - u, far, False)
            return carry

        lax.fori_loop(0, n_main, far_main, 0)
    else:
        n_rest = n_far

    def far_single(t, carry):
        block(n_rest - 1 - t, far, False)
        return carry

    lax.fori_loop(0, n_rest, far_single, 0)

    osb_ref[...] = accs_ref[...].T.astype(osb_ref.dtype)
    o_d = (a1_ref[...] / l1_ref[...] - lam_ref[0] * (a2_ref[...] / l2_ref[...])).T
    od_ref[...] = _sub_rms(o_d, g_ref[...], post_scale).astype(od_ref.dtype)


def _attn_prompt(q_sb, k_sb, v_sb, q_d, k_d, v_d, bias_tiles, far_bias, lam, ut_mat, subln_g,
                 *, batch, seq, heads, bq, post_scale):
    t, w = q_sb.shape
    dh = w // heads
    nq = seq // bq
    far_unroll = 4 if nq > 6 else 1
    kern = functools.partial(_attn_prompt_kernel, bq=bq, nq=nq, dqk=dh // 2, post_scale=post_scale,
                             far_unroll=far_unroll)
    qmap = lambda b, h, i, *_: (b * nq + i, h)
    kvmap = lambda b, h, i, *_: (b, h)
    row_stat = pltpu.VMEM((1, bq), F32)
    grid_spec = pltpu.PrefetchScalarGridSpec(
        num_scalar_prefetch=2,
        grid=(batch, heads, nq),
        in_specs=[pl.BlockSpec((bq, dh), qmap), pl.BlockSpec((seq, dh), kvmap), pl.BlockSpec((seq, dh), kvmap),
                  pl.BlockSpec((bq, dh), qmap), pl.BlockSpec((seq, dh), kvmap), pl.BlockSpec((seq, dh), kvmap),
                  pl.BlockSpec((None, 2, bq, bq), lambda b, h, i, *_: (h, 0, 0, 0)),
                  pl.BlockSpec((bq, bq), lambda b, h, i, *_: (0, 0)),
                  pl.BlockSpec((1, dh), lambda b, h, i, *_: (0, 0))],
        out_specs=[pl.BlockSpec((bq, dh), qmap), pl.BlockSpec((bq, dh), qmap)],
        scratch_shapes=[pltpu.VMEM((nq, bq, dh), BF16), pltpu.VMEM((nq, dh, bq), BF16),
                        pltpu.VMEM((nq, bq, dh), BF16), pltpu.VMEM((nq, dh, bq), BF16),
                        row_stat, pltpu.VMEM((dh, bq), F32),
                        row_stat, row_stat, pltpu.VMEM((dh, bq), F32),
                        row_stat, row_stat, pltpu.VMEM((dh, bq), F32)],
    )
    return pl.pallas_call(
        kern,
        grid_spec=grid_spec,
        out_shape=[jax.ShapeDtypeStruct((t, w), BF16), jax.ShapeDtypeStruct((t, w), BF16)],
        compiler_params=_params(("arbitrary", "arbitrary", "arbitrary")),
        name="attn_prompt",
    )(far_bias, lam, q_sb, k_sb, v_sb, q_d, k_d, v_d, bias_tiles, ut_mat, subln_g)


def _attn_sample_kernel(pt_ref, lam_ref,
                        qs_ref, qd_ref, kns_ref, vns_ref, knd_ref, vnd_ref,
                        bnew_ref, blast_ref, bfar_ref, u_ref, ones_ref, g_ref, *rest,
                        n_pg, heads, dec_seq, page, post_scale):
    pages = rest[:4 * n_pg]
    osb_ref, od_ref = rest[4 * n_pg:4 * n_pg + 2]
    c_ref, accs_ref, m_ref, l_ref, accd_ref, kss_ref, vss_ref, kds_ref, vds_ref = rest[4 * n_pg + 2:]
    ks_pages, vs_pages = pages[0:n_pg], pages[n_pg:2 * n_pg]
    kd_pages, vd_pages = pages[2 * n_pg:3 * n_pg], pages[3 * n_pg:4 * n_pg]
    j = pl.program_id(1)
    nj = pl.num_programs(1)
    rq = BF16_ROWS
    dh = g_ref.shape[1]
    u_mat = u_ref[...]
    ones = ones_ref[...]

    def pv_all(p_list, values):
        full = _dot(jnp.concatenate([p.astype(BF16) for p in p_list], axis=1), values)
        return jnp.concatenate([full[hh * rq:(hh + 1) * rq, hh * dh:(hh + 1) * dh] for hh in range(heads)], axis=0)

    def sb_keys(z, carry, strict):
        sp = _softplus(z)
        log_1m = -sp
        if strict is not None:
            log_1m = jnp.where(strict, log_1m, 0.0)
        hi, lo = _split_bf16(log_1m)
        later = _dot(hi, u_mat) + _dot(lo, u_mat)
        a = jnp.exp((z - sp) + later + carry)
        if strict is not None:
            a = jnp.where(strict, a, 0.0)
        return a, carry + (_dot(hi, ones) + _dot(lo, ones))

    def softmax_update(s_list, values, m_old, l_old, acc_old):
        mx = functools.reduce(jnp.maximum, [jnp.max(s, axis=1, keepdims=True) for s in s_list])
        m_new = jnp.maximum(m_old, mx)
        alpha = jnp.exp(m_old - m_new)
        p_list = [jnp.exp(s - m_new) for s in s_list]
        l_new = alpha * l_old + functools.reduce(
            lambda a, b: a + b, [jnp.sum(p, axis=1, keepdims=True) for p in p_list])
        return m_new, l_new, alpha * acc_old + pv_all(p_list, values)

    def stage(page_refs, dst_ref):
        for p, ref in enumerate(page_refs):
            dst_ref[p * page:(p + 1) * page, :] = jnp.concatenate(
                [ref[pl.ds(hh, page, stride=heads), :] for hh in range(heads)], axis=1).astype(BF16)

    def page_scores(keys_ref, q_ref):
        zt = _dot_t(keys_ref[...], q_ref[...])
        return [zt[p * page:(p + 1) * page, :].T for p in range(n_pg)]

    @pl.when(j == 0)
    def _():
        shape = (heads * rq, kns_ref.shape[0])
        r_i = lax.broadcasted_iota(jnp.int32, shape, 0)
        s_i = lax.broadcasted_iota(jnp.int32, shape, 1)
        a, carry = sb_keys(_dot_t(qs_ref[...], kns_ref[...]), jnp.zeros(shape, F32), s_i < r_i % rq)
        accs_ref[...] = pv_all([a], vns_ref[...])
        c_ref[...] = carry
        sc = _dot_t(qd_ref[...], knd_ref[...]) + bnew_ref[...]
        sc = jnp.where(s_i <= r_i % dec_seq, sc, NEG_BIG)
        m_new, l_new, acc = softmax_update(
            [sc], vnd_ref[...], jnp.full((shape[0], 1), NEG_BIG, F32),
            jnp.zeros((shape[0], 1), F32), jnp.zeros(accd_ref.shape, F32))
        m_ref[...] = m_new
        l_ref[...] = l_new
        accd_ref[...] = acc

    stage(ks_pages, kss_ref)
    stage(vs_pages, vss_ref)
    stage(kd_pages, kds_ref)
    stage(vd_pages, vds_ref)

    carry = c_ref[...]
    a_list = []
    for z in page_scores(kss_ref, qs_ref):
        a, carry = sb_keys(z, carry, None)
        a_list.append(a)
    accs_ref[...] += pv_all(a_list, vss_ref[...])
    c_ref[...] = carry

    s_list = page_scores(kds_ref, qd_ref)
    s_list = [s + (jnp.where(j == 0, blast_ref[...], bfar_ref[...]) if p == 0 else bfar_ref[...])
              for p, s in enumerate(s_list)]
    m_new, l_new, acc = softmax_update(s_list, vds_ref[...], m_ref[...], l_ref[...], accd_ref[...])
    m_ref[...] = m_new
    l_ref[...] = l_new
    accd_ref[...] = acc

    @pl.when(j == nj - 1)
    def _():
        acc_s = accs_ref[...]
        dn = accd_ref[...] / l_ref[...]
        g = g_ref[...]
        for hh in range(heads):
            cs = slice(hh * dh, (hh + 1) * dh)
            osb_ref[:, cs] = acc_s[hh * rq:hh * rq + dec_seq].astype(osb_ref.dtype)
            o_d = dn[hh * rq:hh * rq + dec_seq] - lam_ref[0] * dn[hh * rq + dec_seq:hh * rq + 2 * dec_seq]
            od_ref[:, cs] = _sub_rms(o_d, g, post_scale).astype(od_ref.dtype)


def _attn_sample(page_table, lam, q_sb16, q_d16, new_kv, caches, bias_new, bias_last, bias_far, u_mat, ones_mat,
                 subln_g, *, layer, depth, heads, dec_seq, page, n_pg, post_scale):
    nb, n_pages = page_table.shape
    rows, w = q_sb16.shape[1:]
    dh = w // heads
    nj = n_pages // n_pg
    kern = functools.partial(_attn_sample_kernel, n_pg=n_pg, heads=heads, dec_seq=dec_seq, page=page,
                             post_scale=post_scale)
    bmap = lambda b, j, *_: (b, 0, 0)
    const = lambda b, j, *_: (0, 0)

    def page_spec(p):
        def imap(b, j, pt, lam_):
            return (pt[b * n_pages + (n_pages - 1 - (j * n_pg + p))] * depth + layer, 0, 0)
        return pl.BlockSpec((None, page * heads, dh), imap)

    in_specs = [pl.BlockSpec((None, rows, w), bmap)] * 2
    in_specs += [pl.BlockSpec((None, page, w), bmap)] * 4
    in_specs += [pl.BlockSpec((rows, page), const)] * 3 + [pl.BlockSpec((page, page), const)] * 2
    in_specs += [pl.BlockSpec((1, dh), const)]
    for _ in range(4):
        in_specs += [page_spec(p) for p in range(n_pg)]
    omap = lambda b, j, *_: (b, 0)
    grid_spec = pltpu.PrefetchScalarGridSpec(
        num_scalar_prefetch=2,
        grid=(nb, nj),
        in_specs=in_specs,
        out_specs=[pl.BlockSpec((dec_seq, w), omap), pl.BlockSpec((dec_seq, w), omap)],
        scratch_shapes=[pltpu.VMEM((rows, page), F32), pltpu.VMEM((rows, dh), F32),
                        pltpu.VMEM((rows, 1), F32), pltpu.VMEM((rows, 1), F32), pltpu.VMEM((rows, dh), F32)]
        + [pltpu.VMEM((n_pg * page, w), BF16)] * 4,
    )
    page_args = []
    for c in caches:
        page_args += [c] * n_pg
    return pl.pallas_call(
        kern,
        grid_spec=grid_spec,
        out_shape=[jax.ShapeDtypeStruct((nb * dec_seq, w), F32)] * 2,
        compiler_params=_params(("arbitrary", "arbitrary")),
        name="attn_sample",
    )(page_table.reshape(-1), lam, q_sb16, q_d16, *new_kv, bias_new, bias_last, bias_far, u_mat, ones_mat,
      subln_g, *page_args)


def _out_router_kernel(osb_ref, od_ref, x_ref, g1_ref, sh2_ref, sc2_ref, wo_ref, lng_ref, lnb_ref,
                       wrh_ref, wrl_ref, br_ref, x1_ref, h2_ref, idx_ref, gate_ref, *, wsb, alpha):
    o = (_dot(osb_ref[...].astype(BF16), wo_ref[:wsb, :]) + _dot(od_ref[...].astype(BF16), wo_ref[wsb:, :]))
    x1 = _ln_rows(alpha * x_ref[...] + g1_ref[...] * o) * lng_ref[...] + lnb_ref[...]
    x1_ref[...] = x1
    h2 = _ln_rows(x1) * (1.0 + sc2_ref[...]) + sh2_ref[...]
    _pack_rows(h2, h2_ref, h2.shape[0])
    hi, lo = _split_bf16(h2)
    logits = (_dot(hi, wrh_ref[...]) + _dot(lo, wrh_ref[...]) + _dot(hi, wrl_ref[...])) + br_ref[...]
    lane = lax.broadcasted_iota(jnp.int32, logits.shape, 1)
    lane_f = lane.astype(F32)
    vals, idxs = [], []
    for _ in range(TOP_K):
        mx = jnp.max(logits, axis=1, keepdims=True)
        ix = jnp.min(jnp.where(logits == mx, lane_f, float(logits.shape[1])), axis=1, keepdims=True)
        vals.append(mx)
        idxs.append(ix.astype(jnp.int32))
        logits = jnp.where(lane_f == ix, -jnp.inf, logits)
    exps = [jnp.exp(v - vals[0]) for v in vals]
    denom = functools.reduce(lambda a, b: a + b, exps)
    idx_out = jnp.zeros(lane.shape, jnp.int32)
    gate_out = jnp.zeros(lane.shape, F32)
    for k in range(TOP_K):
        idx_out = jnp.where(lane == k, idxs[k], idx_out)
        gate_out = jnp.where(lane == k, exps[k] / denom, gate_out)
    idx_ref[...] = idx_out
    gate_ref[...] = gate_out


def _out_router(o_sb, o_d, x, mod, mod_spec, w_out_bf, ln_g, ln_b, wr_hi, wr_lo, br_pad, *, tm, alpha):
    t, d = x.shape
    wsb = o_sb.shape[1]
    ne = wr_hi.shape[1]
    nsub = d // (2 * LANES)
    row = lambda i: (i, 0)
    const = lambda i: (0, 0)
    kern = functools.partial(_out_router_kernel, wsb=wsb, alpha=alpha)
    return pl.pallas_call(
        kern,
        grid=(t // tm,),
        in_specs=[pl.BlockSpec((tm, wsb), row), pl.BlockSpec((tm, o_d.shape[1]), row), pl.BlockSpec((tm, d), row),
                  mod_spec(2), mod_spec(3), mod_spec(4),
                  pl.BlockSpec(w_out_bf.shape, const, pipeline_mode=pl.Buffered(1)),
                  pl.BlockSpec((1, d), const), pl.BlockSpec((1, d), const),
                  pl.BlockSpec((d, ne), const), pl.BlockSpec((d, ne), const), pl.BlockSpec((1, ne), const)],
        out_specs=[pl.BlockSpec((tm, d), row), pl.BlockSpec((tm * nsub, LANES), row),
                   pl.BlockSpec((tm, ne), row), pl.BlockSpec((tm, ne), row)],
        out_shape=[jax.ShapeDtypeStruct((t, d), F32), jax.ShapeDtypeStruct((t * nsub, LANES), U32),
                   jax.ShapeDtypeStruct((t, ne), jnp.int32), jax.ShapeDtypeStruct((t, ne), F32)],
        compiler_params=_params(("arbitrary",)),
        name="out_router",
    )(o_sb, o_d, x, mod, mod, mod, w_out_bf, ln_g, ln_b, wr_hi, wr_lo, br_pad)


def _row_copy(src_ref, src_row, dst_ref, dst_row, nsub, sem):
    def rows(r):
        return pl.ds(r * nsub if isinstance(r, int) else pl.multiple_of(r * nsub, nsub), nsub)

    return pltpu.make_async_copy(src_ref.at[rows(src_row)], dst_ref.at[rows(dst_row)], sem)


def _dispatch_kernel(rowp_ref, x_ref, init_ref, dst_ref, sem, *, chunk, top_k, nsub):
    del init_ref
    n_tok = chunk // top_k

    def start(t, carry):
        for k in range(top_k):
            _row_copy(x_ref, t, dst_ref, rowp_ref[t * top_k + k], nsub, sem).start()
        return carry

    lax.fori_loop(0, n_tok, start, 0)
    for _ in range(top_k):
        pltpu.make_async_copy(x_ref, dst_ref.at[pl.ds(0, n_tok * nsub)], sem).wait()


def _dispatch(row_p, h2_packed, n_rows, *, chunk, top_k, nsub):
    n_pairs = row_p.shape[0]
    init = jnp.zeros((n_rows * nsub, LANES), U32)
    kern = functools.partial(_dispatch_kernel, chunk=chunk, top_k=top_k, nsub=nsub)
    return pl.pallas_call(
        kern,
        grid=(n_pairs // chunk,),
        in_specs=[pl.BlockSpec((chunk,), lambda i: (i,), memory_space=pltpu.SMEM),
                  pl.BlockSpec((chunk // top_k * nsub, LANES), lambda i: (i, 0)),
                  pl.BlockSpec(memory_space=pl.ANY)],
        out_specs=pl.BlockSpec(memory_space=pl.ANY),
        out_shape=jax.ShapeDtypeStruct(init.shape, U32),
        scratch_shapes=[pltpu.SemaphoreType.DMA],
        input_output_aliases={2: 0},
        compiler_params=_params(("arbitrary",)),
        name="dispatch",
    )(row_p, h2_packed, init)


def _moe_up_kernel(te_ref, tf_ref, rb_ref, nu_ref, x_ref, w_ref, b_ref, perm_ref, o_ref, wb_ref, *, tm, nsub):
    i = pl.program_id(1)

    @pl.when(tf_ref[i] == 1)
    def _():
        wb_ref[...] = w_ref[...].astype(BF16)

    @pl.when(i < nu_ref[0])
    def _():
        hi, lo = _unpack_rows(x_ref, 0, tm, nsub)
        x = jnp.concatenate(hi + lo, axis=1).astype(BF16)
        u = _dot(x, wb_ref[...]) + b_ref[...]
        even = (lax.broadcasted_iota(jnp.int32, (tm, LANES), 1) & 1) == 0
        perm = perm_ref[...]
        for c in range(u.shape[1] // (2 * LANES)):
            a = u[:, 2 * c * LANES:(2 * c + 1) * LANES]
            b = u[:, (2 * c + 1) * LANES:(2 * c + 2) * LANES]
            glu = jnp.where(even, a, pltpu.roll(b, 1, 1))
            lin = jnp.where(even, pltpu.roll(a, LANES - 1, 1), b)
            glu = jnp.minimum(glu, SWIGLU_LIMIT)
            lin = jnp.clip(lin, -SWIGLU_LIMIT, SWIGLU_LIMIT)
            act = (glu * jax.nn.sigmoid(SWIGLU_ALPHA * glu) * (lin + 1.0)).astype(BF16)
            o_ref[:, c * LANES:(c + 1) * LANES] = _dot(act, perm).astype(o_ref.dtype)


def _moe_up(tile_expert, tile_first, row_block, n_used, xs, w_up, b_up, perm, *, tm, tf, nsub):
    r_pad = xs.shape[0] // nsub
    ne, d, f2 = w_up.shape
    f = f2 // 2
    n_tiles = r_pad // tm
    kern = functools.partial(_moe_up_kernel, tm=tm, nsub=nsub)
    grid_spec = pltpu.PrefetchScalarGridSpec(
        num_scalar_prefetch=4,
        grid=(f // tf, n_tiles),
        in_specs=[pl.BlockSpec((tm * nsub, LANES), lambda j, i, te, tfr, rb, nu: (rb[i], 0)),
                  pl.BlockSpec((None, d, 2 * tf), lambda j, i, te, tfr, rb, nu: (te[i], 0, j)),
                  pl.BlockSpec((None, 1, 2 * tf), lambda j, i, te, tfr, rb, nu: (te[i], 0, j)),
                  pl.BlockSpec((LANES, LANES), lambda j, i, te, tfr, rb, nu: (0, 0))],
        out_specs=pl.BlockSpec((tm, tf), lambda j, i, te, tfr, rb, nu: (rb[i], j)),
        scratch_shapes=[pltpu.VMEM((d, 2 * tf), BF16)],
    )
    return pl.pallas_call(
        kern,
        grid_spec=grid_spec,
        out_shape=jax.ShapeDtypeStruct((r_pad, f), BF16),
        compiler_params=_params(("arbitrary", "arbitrary")),
        name="moe_up",
    )(tile_expert, tile_first, row_block, n_used, xs, w_up, b_up.reshape(ne, 1, f2), perm)


def _moe_down_kernel(te_ref, tf_ref, rb_ref, nu_ref, a_ref, w_ref, b_ref, o_ref, wb_ref, *, tm):
    i = pl.program_id(0)

    @pl.when(tf_ref[i] == 1)
    def _():
        wb_ref[...] = w_ref[...].astype(BF16)

    @pl.when(i < nu_ref[0])
    def _():
        _pack_rows(_dot(a_ref[...], wb_ref[...]) + b_ref[...], o_ref, tm)


def _moe_down(tile_expert, tile_first, row_block, n_used, act, w_down, b_down, *, tm, nsub):
    r_pad, f = act.shape
    ne, _, d = w_down.shape
    n_tiles = r_pad // tm
    grid_spec = pltpu.PrefetchScalarGridSpec(
        num_scalar_prefetch=4,
        grid=(n_tiles,),
        in_specs=[pl.BlockSpec((tm, f), lambda i, te, tfr, rb, nu: (rb[i], 0)),
                  pl.BlockSpec((None, f, d), lambda i, te, tfr, rb, nu: (te[i], 0, 0)),
                  pl.BlockSpec((None, 1, d), lambda i, te, tfr, rb, nu: (te[i], 0, 0))],
        out_specs=pl.BlockSpec((tm * nsub, LANES), lambda i, te, tfr, rb, nu: (rb[i], 0)),
        scratch_shapes=[pltpu.VMEM((f, d), BF16)],
    )
    return pl.pallas_call(
        functools.partial(_moe_down_kernel, tm=tm),
        grid_spec=grid_spec,
        out_shape=jax.ShapeDtypeStruct((r_pad * nsub, LANES), U32),
        compiler_params=_params(("arbitrary",)),
        name="moe_down",
    )(tile_expert, tile_first, row_block, n_used, act, w_down, b_down.reshape(ne, 1, d))


def _route(top_i, n_exp, tm, n_tiles):
    flat_e = top_i.reshape(-1)
    onehot = (flat_e[:, None] == jnp.arange(n_exp, dtype=jnp.int32)[None, :]).astype(jnp.int32)
    csum = jnp.cumsum(onehot, axis=0)
    rank = jnp.sum((csum - onehot) * onehot, axis=1)
    counts = csum[-1]
    tiles_e = (counts + tm - 1) // tm
    tile_end = jnp.cumsum(tiles_e)
    tile_start = tile_end - tiles_e
    n_used = tile_end[-1]
    row_p = (jnp.sum(onehot * tile_start[None, :], axis=1) * tm + rank).astype(jnp.int32)
    tile_ids = jnp.arange(n_tiles, dtype=jnp.int32)
    clamped = jnp.minimum(tile_ids, n_used - 1)
    tile_expert = jnp.minimum(jnp.sum((tile_end[None, :] <= clamped[:, None]).astype(jnp.int32), axis=1), n_exp - 1)
    tile_first = (jnp.any(tile_ids[:, None] == tile_start[None, :], axis=1) & (tile_ids < n_used)).astype(jnp.int32)
    return (tile_expert.astype(jnp.int32), tile_first, clamped.astype(jnp.int32),
            n_used.reshape(1).astype(jnp.int32), row_p)


def _combine_kernel(pos_ref, x1_ref, gate_ref, g2_ref, lng_ref, lnb_ref, y_ref, o_ref, buf_ref, sem,
                    *, tm, top_k, nsub, alpha):
    def start(t, carry):
        for k in range(top_k):
            _row_copy(y_ref, pos_ref[t * top_k + k], buf_ref, k * tm + t, nsub, sem).start()
        return carry

    lax.fori_loop(0, tm, start, 0)
    pltpu.make_async_copy(y_ref.at[pl.ds(0, tm * top_k * nsub)], buf_ref, sem).wait()

    gates = gate_ref[...]
    f_hi = [jnp.zeros((tm, LANES), F32)] * nsub
    f_lo = [jnp.zeros((tm, LANES), F32)] * nsub
    for k in range(top_k):
        gk = gates[:, k:k + 1]
        hi, lo = _unpack_rows(buf_ref, k * tm * nsub, tm, nsub)
        f_hi = [acc + gk * v for acc, v in zip(f_hi, hi)]
        f_lo = [acc + gk * v for acc, v in zip(f_lo, lo)]
    f = jnp.concatenate(f_hi + f_lo, axis=1)
    y = alpha * x1_ref[...] + g2_ref[...] * f
    o_ref[...] = _ln_rows(y) * lng_ref[...] + lnb_ref[...]


def _combine_ln(pos, tile_offset, x1, gates, mod, mod_spec, ln_g, ln_b, y_rows, *, tm, top_k, nsub, alpha):
    t, d = x1.shape
    ne = gates.shape[1]
    row = lambda i: (i, 0)
    const = lambda i: (0, 0)
    kern = functools.partial(_combine_kernel, tm=tm, top_k=top_k, nsub=nsub, alpha=alpha)
    return pl.pallas_call(
        kern,
        grid=(t // tm,),
        in_specs=[pl.BlockSpec((tm * top_k,), lambda i: (i + tile_offset,), memory_space=pltpu.SMEM),
                  pl.BlockSpec((tm, d), row), pl.BlockSpec((tm, ne), lambda i: (i + tile_offset, 0)), mod_spec(5),
                  pl.BlockSpec((1, d), const), pl.BlockSpec((1, d), const),
                  pl.BlockSpec(memory_space=pl.ANY)],
        out_specs=pl.BlockSpec((tm, d), row),
        out_shape=jax.ShapeDtypeStruct((t, d), F32),
        scratch_shapes=[pltpu.VMEM((tm * top_k * nsub, LANES), U32), pltpu.SemaphoreType.DMA],
        compiler_params=_params(("arbitrary",)),
        name="combine_ln",
    )(pos, x1, gates, mod, ln_g, ln_b, y_rows)


def _t5_bucket(rel, n_buckets):
    n = jnp.maximum(rel, 0)
    max_exact = n_buckets // 2
    nf = jnp.maximum(n, 1).astype(F32)
    large = max_exact + (jnp.log(nf / max_exact) / math.log(MAX_DISTANCE / max_exact)
                         * (n_buckets - max_exact)).astype(jnp.int32)
    large = jnp.minimum(large, n_buckets - 1)
    return jnp.where(n < max_exact, n, large)


def _bias_tile(rel, rel_bias):
    n_buckets = rel_bias.shape[0]
    onehot = (_t5_bucket(rel, n_buckets)[..., None] == jnp.arange(n_buckets)).astype(F32)
    return jnp.einsum("rsk,kh->hrs", onehot, rel_bias.astype(F32), precision=lax.Precision.HIGHEST)


def _sample_queries(q, nb, dec_seq, heads, maps):
    dh = q.shape[1] // heads
    q4 = q.reshape(nb, dec_seq, heads, dh).transpose(0, 2, 1, 3)
    lane = jnp.arange(dh)
    parts = [jnp.where((lane >= lo) & (lane < hi), q4, jnp.zeros_like(q4)) for lo, hi in maps]
    q16 = jnp.concatenate(parts, axis=2)
    q16 = jnp.pad(q16, ((0, 0), (0, 0), (0, BF16_ROWS - q16.shape[2]), (0, 0)))
    eye = jnp.eye(heads, dtype=q.dtype)
    out = q16[:, :, :, None, :] * eye[None, :, None, :, None]
    return out.reshape(nb, heads * BF16_ROWS, heads * dh)


def kernel(x_prompt, x_sample, cache_k_sb, cache_v_sb, cache_k_diff, cache_v_diff, page_table, c_prompt, c_sample,
           w_ada, b_ada, w_in, w_out, rel_bias, lambda_q1, lambda_k1, lambda_q2, lambda_k2, subln_g, ln1_g, ln1_b,
           w_router, b_router, w_up, b_up, w_down, b_down, ln2_g, ln2_b):
    bp, seq, d = x_prompt.shape
    bs, dec_seq, _ = x_sample.shape
    n_pool, page, depth, h_sb, d_sb = cache_k_sb.shape
    _, _, _, h_d, d_vd = cache_k_diff.shape
    n_pages = page_table.shape[1]
    wsb, wd = h_sb * d_sb, h_d * d_vd
    n_exp = w_router.shape[-1]
    n_buckets = rel_bias.shape[0]
    d_ff = w_down.shape[2]
    alpha = (2.0 * depth) ** 0.25
    tp, ts = bp * seq, bs * dec_seq
    assert h_sb == h_d and d_sb == d_vd == LANES and page >= MAX_DISTANCE and 2 * dec_seq == BF16_ROWS
    heads = h_sb
    nsub = d // (2 * LANES)

    tm_p = min(256, seq)
    tm_s = ts
    bq = min(512, seq)
    n_pg = min(8, n_pages)
    tm_e = min(256, ts)
    tf_up = min(1024, d_ff)
    chunk = math.gcd((tp + ts) * TOP_K, 1024)
    assert seq % bq == 0 and bq >= MAX_DISTANCE and n_pages % n_pg == 0
    assert tp % tm_e == 0 and ts % tm_e == 0

    xp = x_prompt.reshape(tp, d)
    xs = x_sample.reshape(ts, d)
    c_all = jnp.concatenate([c_prompt, c_sample, jnp.zeros(((-(bp + bs)) % BF16_ROWS, d), F32)], axis=0)

    caches = [jnp.swapaxes(c, 1, 2).reshape(n_pool * depth, page * heads, d_sb) if depth > 1
              else c.reshape(n_pool, page * heads, d_sb)
              for c in (cache_k_sb, cache_v_sb, cache_k_diff, cache_v_diff)]

    ar = jnp.arange
    rel_t = ar(bq)[None, :] - ar(bq)[:, None]
    bias_prompt = jnp.stack([_bias_tile(rel_t, rel_bias), _bias_tile(rel_t + bq, rel_bias)], axis=1)
    far_bias = rel_bias[n_buckets - 1].astype(F32)
    rows16 = heads * BF16_ROWS
    t_rows = jnp.tile(ar(dec_seq), rows16 // dec_seq)
    h_rows = jnp.repeat(ar(heads), BF16_ROWS)
    pick = lambda tile: tile[h_rows, ar(rows16)]
    bias_new = pick(_bias_tile(t_rows[:, None] - ar(page)[None, :], rel_bias))
    bias_last = pick(_bias_tile(page + t_rows[:, None] - ar(page)[None, :], rel_bias))
    bias_far = jnp.broadcast_to(far_bias[h_rows][:, None], (rows16, page))
    later = lambda n: (ar(n)[:, None] > ar(n)[None, :]).astype(BF16)
    ut_blk = later(bq).T
    u_page = later(page)
    ones_page = jnp.ones((page, page), BF16)
    lane = ar(LANES)
    perm = (lane[None, :] == (lane[:, None] // 2 + (LANES // 2) * (lane[:, None] % 2))).astype(BF16)

    w_in_bf = w_in.astype(BF16)
    w_out_bf = w_out.astype(BF16)

    prompt_rows, sample_rows = [], []
    for l in range(depth):
        ada = _ada(c_all, w_ada[l], b_ada[l])
        mod_p = ada[:bp].reshape(bp, 1, 6 * d)
        mod_s = jnp.repeat(ada[bp:bp + bs], dec_seq, axis=0).reshape(1, ts, 6 * d)
        tiles_per_b = seq // tm_p
        spec_p = lambda k: pl.BlockSpec((None, 1, d), lambda i, k=k: (i // tiles_per_b, 0, k))
        spec_s = lambda k: pl.BlockSpec((None, tm_s, d), lambda i, k=k: (0, i, k))

        lam_init = 0.8 - 0.6 * math.exp(-0.3 * l)
        lam = (jnp.exp(jnp.sum(lambda_q1[l].astype(F32) * lambda_k1[l].astype(F32)))
               - jnp.exp(jnp.sum(lambda_q2[l].astype(F32) * lambda_k2[l].astype(F32))) + lam_init).reshape(1)
        g_sub = subln_g[l].reshape(1, d_vd).astype(F32)
        post = 1.0 - lam_init

        proj_kw = dict(wsb=wsb, wd=wd, sb_scale=d_sb ** -0.5, d_scale=(d_vd // 2) ** -0.5)
        qsb_p, ksb_p, vsb_p, qd_p, kd_p, vd_p = _ln_proj(xp, mod_p, spec_p, w_in_bf[l], tm=tm_p, **proj_kw)
        qsb_s, ksb_s, vsb_s, qd_s, kd_s, vd_s = _ln_proj(xs, mod_s, spec_s, w_in_bf[l], tm=tm_s, **proj_kw)
        prompt_rows.append((ksb_p, vsb_p, kd_p, vd_p))
        sample_rows.append((ksb_s, vsb_s, kd_s, vd_s))

        osb_p, od_p = _attn_prompt(qsb_p, ksb_p, vsb_p, qd_p, kd_p, vd_p, bias_prompt, far_bias, lam, ut_blk, g_sub,
                                   batch=bp, seq=seq, heads=heads, bq=bq, post_scale=post)

        pad_new = lambda a: jnp.pad(a.reshape(bs, dec_seq, -1).astype(BF16), ((0, 0), (0, page - dec_seq), (0, 0)))
        q_sb16 = _sample_queries(qsb_s, bs, dec_seq, heads, [(0, d_sb)])
        q_d16 = _sample_queries(qd_s, bs, dec_seq, heads, [(0, d_vd // 2), (d_vd // 2, d_vd)])
        osb_s, od_s = _attn_sample(page_table, lam, q_sb16, q_d16,
                                   [pad_new(a) for a in (ksb_s, vsb_s, kd_s, vd_s)], caches,
                                   bias_new, bias_last, bias_far, u_page, ones_page, g_sub,
                                   layer=l, depth=depth, heads=heads, dec_seq=dec_seq, page=page, n_pg=n_pg,
                                   post_scale=post)

        ne_pad = LANES
        wr = jnp.pad(w_router[l].astype(F32), ((0, 0), (0, ne_pad - n_exp)))
        wr_hi = wr.astype(BF16)
        wr_lo = (wr - wr_hi.astype(F32)).astype(BF16)
        br_pad = jnp.pad(b_router[l].astype(F32), (0, ne_pad - n_exp), constant_values=NEG_BIG).reshape(1, ne_pad)
        ln1 = (ln1_g[l].reshape(1, d), ln1_b[l].reshape(1, d))
        x1_p, h2_p, idx_p, gate_p = _out_router(osb_p, od_p, xp, mod_p, spec_p, w_out_bf[l], *ln1,
                                                wr_hi, wr_lo, br_pad, tm=tm_p, alpha=alpha)
        x1_s, h2_s, idx_s, gate_s = _out_router(osb_s, od_s, xs, mod_s, spec_s, w_out_bf[l], *ln1,
                                                wr_hi, wr_lo, br_pad, tm=tm_s, alpha=alpha)

        h2 = jnp.concatenate([h2_p, h2_s], axis=0)
        top_i = jnp.concatenate([idx_p[:, :TOP_K], idx_s[:, :TOP_K]], axis=0)
        gates = jnp.concatenate([gate_p, gate_s], axis=0)
        n_tiles = (tp + ts) * TOP_K // tm_e + n_exp
        tile_expert, tile_first, row_block, n_used, row_p = _route(top_i, n_exp, tm_e, n_tiles)
        x_rows = _dispatch(row_p, h2, n_tiles * tm_e, chunk=chunk, top_k=TOP_K, nsub=nsub)
        act = _moe_up(tile_expert, tile_first, row_block, n_used, x_rows, w_up[l], b_up[l], perm,
                      tm=tm_e, tf=tf_up, nsub=nsub)
        y_rows = _moe_down(tile_expert, tile_first, row_block, n_used, act, w_down[l], b_down[l], tm=tm_e, nsub=nsub)

        ln2 = (ln2_g[l].reshape(1, d), ln2_b[l].reshape(1, d))
        comb_kw = dict(tm=tm_e, top_k=TOP_K, nsub=nsub, alpha=alpha)
        tiles_per_b_e = seq // tm_e
        spec_pe = lambda k: pl.BlockSpec((None, 1, d), lambda i, k=k: (i // tiles_per_b_e, 0, k))
        spec_se = lambda k: pl.BlockSpec((None, tm_e, d), lambda i, k=k: (0, i, k))
        xp = _combine_ln(row_p, 0, x1_p, gates, mod_p, spec_pe, *ln2, y_rows, **comb_kw)
        xs = _combine_ln(row_p, tp // tm_e, x1_s, gates, mod_s, spec_se, *ln2, y_rows, **comb_kw)

    def stack_rows(rows, i, nb, t, h, dh):
        return jnp.stack([r[i].reshape(nb, t, h, dh) for r in rows], axis=2)

    return (xp.reshape(bp, seq, d), xs.reshape(bs, dec_seq, d),
            stack_rows(prompt_rows, 0, bp, seq, h_sb, d_sb), stack_rows(prompt_rows, 1, bp, seq, h_sb, d_sb),
            stack_rows(prompt_rows, 2, bp, seq, h_d, d_vd), stack_rows(prompt_rows, 3, bp, seq, h_d, d_vd),
            stack_rows(sample_rows, 0, bs, dec_seq, h_sb, d_sb), stack_rows(sample_rows, 1, bs, dec_seq, h_sb, d_sb),
            stack_rows(sample_rows, 2, bs, dec_seq, h_d, d_vd), stack_rows(sample_rows, 3, bs, dec_seq, h_d, d_vd))
```
